```python
import jax, jax.numpy as jnp
from jax import lax
import numpy as np

D_MODEL = 1024
BATCH = 2
SEQ = 8192
DEPTH = 1

HEAD_DIM = 64
ATTN_GROUPS = ((128, 1), (512, 4), (2048, 16))
ATTN_HEADS_PER_GROUP = 4
ATTN_HEADS = ATTN_HEADS_PER_GROUP * len(ATTN_GROUPS)
ATTN_WIDTH = ATTN_HEADS * HEAD_DIM
ATTN_OUT_WIDTH = ATTN_HEADS_PER_GROUP * HEAD_DIM
ATTN_BLOCK = 128
ROPE_THETA = 500000.0
ROPE_DIM = HEAD_DIM // 4
RWKV_HEADS = D_MODEL // HEAD_DIM
RWKV_WIDTH = RWKV_HEADS * HEAD_DIM
DECAY_LORA = 64
ICLR_LORA = 64
GATE_LORA = 128
D_FF = 4 * D_MODEL
NORM_EPS = 1e-6
GN_EPS = 64e-5
NEG_INF = -1e30
SPLITS = (ATTN_WIDTH, ATTN_WIDTH, ATTN_WIDTH,
          RWKV_WIDTH, RWKV_WIDTH, RWKV_WIDTH,
          DECAY_LORA, ICLR_LORA, GATE_LORA,
          D_MODEL, D_MODEL)
ATTN_QKV_WIDTH = 3 * ATTN_WIDTH
SHIFT_WIDTH = 3 * RWKV_WIDTH + DECAY_LORA + ICLR_LORA + GATE_LORA
IN_WIDTH = sum(SPLITS)

kernel_name = "hybrid_dilated_attn_rwkv7_block"


def rms_norm(t, gain):
    t32 = t.astype(jnp.float32)
    y = t32 * lax.rsqrt(jnp.mean(t32 * t32, axis=-1, keepdims=True) + NORM_EPS)
    return (y * gain.astype(jnp.float32)).astype(t.dtype)


def partial_rope(t, positions):
    half = ROPE_DIM // 2
    inv = ROPE_THETA ** (-jnp.arange(half, dtype=jnp.float32) * 2.0 / ROPE_DIM)
    ang = positions.astype(jnp.float32)[..., None] * inv
    cos = jnp.cos(ang)[:, :, None, :]
    sin = jnp.sin(ang)[:, :, None, :]
    t32 = t.astype(jnp.float32)
    x1 = t32[..., :half]
    x2 = t32[..., half:ROPE_DIM]
    out = jnp.concatenate([x1 * cos - x2 * sin, x1 * sin + x2 * cos, t32[..., ROPE_DIM:]], axis=-1)
    return out.astype(t.dtype)


def dilated_band_attention(q, k, v, dilation, span):
    B, S, H, Dh = q.shape
    unit = dilation * ATTN_BLOCK
    Sp = -(-S // unit) * unit
    L = Sp // dilation
    nb = L // ATTN_BLOCK

    def to_blocks(t):
        t = jnp.pad(t.astype(jnp.float32), ((0, 0), (0, Sp - S), (0, 0), (0, 0)))
        t = t.reshape(B, L, dilation, H, Dh).transpose(0, 2, 1, 3, 4)
        return t.reshape(B, dilation, nb, ATTN_BLOCK, H, Dh)

    qb, kb, vb = to_blocks(q), to_blocks(k), to_blocks(v)

    def with_prev(t):
        prev = jnp.pad(t[:, :, :-1], ((0, 0), (0, 0), (1, 0), (0, 0), (0, 0), (0, 0)))
        return jnp.concatenate([prev, t], axis=3)

    kw, vw = with_prev(kb), with_prev(vb)
    s = jnp.einsum('bdnqhe,bdnkhe->bdnhqk', qb, kw) * (HEAD_DIM ** -0.5)
    qi = jnp.arange(ATTN_BLOCK)[:, None]
    kj = jnp.arange(2 * ATTN_BLOCK)[None, :]
    lag = qi + ATTN_BLOCK - kj
    band = (lag >= 0) & (lag <= span)
    has_prev = (jnp.arange(nb) > 0)[:, None, None] | (kj >= ATTN_BLOCK)[None]
    mask = band[None] & has_prev
    s = jnp.where(mask[None, None, :, None], s, NEG_INF)
    m = jnp.max(s, axis=-1, keepdims=True)
    p = jnp.exp(s - m)
    l = jnp.sum(p, axis=-1, keepdims=True)
    o = jnp.einsum('bdnhqk,bdnkhe->bdnqhe', p / l, vw)
    lse = jnp.swapaxes((m + jnp.log(l))[..., 0], -1, -2)

    def from_blocks(t):
        t = t.reshape((B, dilation, L) + t.shape[4:])
        t = jnp.swapaxes(t, 1, 2)
        return t.reshape((B, Sp) + t.shape[3:])[:, :S]

    return from_blocks(o), from_blocks(lse)


def dilated_attention_mixer(q, k, v, positions):
    B, S = q.shape[:2]
    q = partial_rope(q, positions)
    k = partial_rope(k, positions)
    outs, lses = [], []
    for g, (window, dilation) in enumerate(ATTN_GROUPS):
        sl = slice(g * ATTN_HEADS_PER_GROUP, (g + 1) * ATTN_HEADS_PER_GROUP)
        o, lse = dilated_band_attention(q[:, :, sl], k[:, :, sl], v[:, :, sl], dilation, window // dilation)
        outs.append(o)
        lses.append(lse)
    o = jnp.stack(outs)
    w = jax.nn.softmax(jnp.stack(lses), axis=0)
    out = jnp.sum(w[..., None] * o, axis=0)
    return out.reshape(B, S, ATTN_OUT_WIDTH).astype(q.dtype)


def token_shift(u, mu):
    prev = jnp.pad(u, ((0, 0), (1, 0), (0, 0)))[:, :-1]
    return u + (prev - u) * mu


def wkv7_scan(r, w, k, v, a, b):
    B, S, H, N = r.shape
    xs = tuple(jnp.moveaxis(t, 1, 0) for t in (r, w, k, v, a, b))

    def step(state, inp):
        r_t, w_t, k_t, v_t, a_t, b_t = inp
        sa = jnp.einsum('bhvk,bhk->bhv', state, a_t)
        state = (state * w_t[:, :, None, :] + sa[..., None] * b_t[:, :, None, :]
                 + v_t[..., None] * k_t[:, :, None, :])
        return state, jnp.einsum('bhvk,bhk->bhv', state, r_t)

    s0 = jnp.zeros((B, H, N, N), jnp.float32)
    _, ys = lax.scan(step, s0, xs)
    return jnp.moveaxis(ys, 0, 1)


def rwkv7_mixer(r, k, v, dw, da, dg, w0, w2, a0, a2, g2, k_k, k_a, r_k, gn_w, gn_b):
    B, S, C = r.shape
    H, N = RWKV_HEADS, HEAD_DIM
    f32 = jnp.float32
    w_log = -jax.nn.softplus(-(w0 + jnp.tanh(dw) @ w2).astype(f32)) - 0.5
    decay = jnp.exp(-jnp.exp(w_log))
    a = jax.nn.sigmoid((a0 + da @ a2).astype(f32))
    g = (jax.nn.sigmoid(dg) @ g2).astype(f32)
    kk = (k * k_k).astype(f32).reshape(B, S, H, N)
    kk = kk / jnp.maximum(jnp.sqrt(jnp.sum(kk * kk, axis=-1, keepdims=True)), 1e-12)
    k = k.astype(f32) * (1.0 + (a - 1.0) * k_a.astype(f32))
    heads = lambda t: t.astype(f32).reshape(B, S, H, N)
    rh, kh, vh, wh, ah = heads(r), heads(k), heads(v), heads(decay), heads(a)
    y = wkv7_scan(rh, wh, kh, vh, -kk, kk * ah)
    mu = jnp.mean(y, axis=-1, keepdims=True)
    var = jnp.mean(jnp.square(y - mu), axis=-1, keepdims=True)
    y = ((y - mu) * lax.rsqrt(var + GN_EPS)).reshape(B, S, C) * gn_w + gn_b
    bonus = jnp.sum(rh * kh * r_k.astype(f32), axis=-1, keepdims=True) * vh
    y = y + bonus.reshape(B, S, C)
    return (y * g).astype(r.dtype)


def setup_inputs(seed: int = 0) -> dict:
    key = jax.random.key(seed)
    ks = jax.random.split(key, 32)
    f32 = jnp.float32
    L, D = DEPTH, D_MODEL
    nrm = lambda kk, shape, scale: jax.random.normal(kk, shape, f32) * scale
    x = nrm(ks[0], (BATCH, SEQ, D), 1.0)
    c = nrm(ks[1], (BATCH, D), 1.0)
    offset = jax.random.randint(ks[2], (BATCH, 1), 0, 4096, dtype=jnp.int32)
    positions = offset + jnp.arange(SEQ, dtype=jnp.int32)[None, :]
    return {
        "x": x,
        "c": c,
        "positions": positions,
        "ada_w": nrm(ks[3], (L, D, 6 * D), 0.5 * D ** -0.5),
        "ada_b": nrm(ks[4], (L, 6 * D), 0.02),
        "norm_mix_pre": 1.0 + nrm(ks[5], (L, D), 0.05),
        "norm_mix_post": 1.0 + nrm(ks[6], (L, D), 0.05),
        "norm_ffn_pre": 1.0 + nrm(ks[7], (L, D), 0.05),
        "norm_ffn_post": 1.0 + nrm(ks[8], (L, D), 0.05),
        "w_in": nrm(ks[9], (L, D, IN_WIDTH), D ** -0.5),
        "shift_mu": jax.random.uniform(ks[10], (L, SHIFT_WIDTH), f32),
        "decay_w0": jax.random.uniform(ks[11], (L, RWKV_WIDTH), f32, -6.0, 1.0),
        "decay_w2": nrm(ks[12], (L, DECAY_LORA, RWKV_WIDTH), DECAY_LORA ** -0.5),
        "iclr_a0": nrm(ks[13], (L, RWKV_WIDTH), 0.1),
        "iclr_a2": nrm(ks[14], (L, ICLR_LORA, RWKV_WIDTH), ICLR_LORA ** -0.5),
        "gate_g2": nrm(ks[15], (L, GATE_LORA, RWKV_WIDTH), GATE_LORA ** -0.5),
        "k_k": 0.85 + nrm(ks[16], (L, RWKV_WIDTH), 0.05),
        "k_a": 1.0 + nrm(ks[17], (L, RWKV_WIDTH), 0.05),
        "r_k": nrm(ks[18], (L, RWKV_HEADS, HEAD_DIM), 0.1),
        "gn_w": 1.0 + nrm(ks[19], (L, RWKV_WIDTH), 0.05),
        "gn_b": nrm(ks[20], (L, RWKV_WIDTH), 0.02),
        "w_branch": jnp.concatenate([nrm(ks[21], (L, ATTN_OUT_WIDTH, D), ATTN_OUT_WIDTH ** -0.5),
                                     nrm(ks[22], (L, RWKV_WIDTH, D), RWKV_WIDTH ** -0.5)], axis=1),
        "w_out": nrm(ks[23], (L, D, D), D ** -0.5),
        "w_ff1": nrm(ks[24], (L, D, D_FF), D ** -0.5),
        "w_ff2": nrm(ks[25], (L, D_FF, D), D_FF ** -0.5),
    }


def reference(x, c, positions, ada_w, ada_b, norm_mix_pre, norm_mix_post, norm_ffn_pre,
              norm_ffn_post, w_in, shift_mu, decay_w0, decay_w2, iclr_a0, iclr_a2, gate_g2,
              k_k, k_a, r_k, gn_w, gn_b, w_branch, w_out, w_ff1, w_ff2):
    B, S, D = x.shape
    for l in range(DEPTH):
        mod = jax.nn.silu(c) @ ada_w[l] + ada_b[l]
        sh1, sc1, gt1, sh2, sc2, gt2 = [m[:, None, :] for m in jnp.split(mod, 6, axis=-1)]

        h = rms_norm(x, norm_mix_pre[l]) * (1.0 + sc1) + sh1
        proj = h @ w_in[l]
        attn_part = proj[..., :ATTN_QKV_WIDTH]
        rwkv_part = token_shift(proj[..., ATTN_QKV_WIDTH:ATTN_QKV_WIDTH + SHIFT_WIDTH], shift_mu[l])
        gate_part = proj[..., ATTN_QKV_WIDTH + SHIFT_WIDTH:]
        q, k, v = [t.reshape(B, S, ATTN_HEADS, HEAD_DIM) for t in jnp.split(attn_part, 3, axis=-1)]
        cuts = np.cumsum([RWKV_WIDTH, RWKV_WIDTH, RWKV_WIDTH, DECAY_LORA, ICLR_LORA])
        r_r, r_k_, r_v, dw, da, dg = jnp.split(rwkv_part, cuts, axis=-1)
        g_attn, g_rwkv = jnp.split(gate_part, 2, axis=-1)

        o_attn = dilated_attention_mixer(q, k, v, positions)
        o_rwkv = rwkv7_mixer(r_r, r_k_, r_v, dw, da, dg, decay_w0[l], decay_w2[l], iclr_a0[l],
                             iclr_a2[l], gate_g2[l], k_k[l], k_a[l], r_k[l], gn_w[l], gn_b[l])
        wb = w_branch[l]
        merged = (jax.nn.sigmoid(g_attn) * (o_attn @ wb[:ATTN_OUT_WIDTH])
                  + jax.nn.sigmoid(g_rwkv) * (o_rwkv @ wb[ATTN_OUT_WIDTH:]))
        mix_out = merged @ w_out[l]
        x = (x + gt1 * rms_norm(mix_out, norm_mix_post[l])).astype(x.dtype)

        h = rms_norm(x, norm_ffn_pre[l]) * (1.0 + sc2) + sh2
        f = jnp.square(jax.nn.relu(h @ w_ff1[l])) @ w_ff2[l]
        x = (x + gt2 * rms_norm(f, norm_ffn_post[l])).astype(x.dtype)
    return x
```

```python
import functools

import jax
import jax.numpy as jnp
from jax import lax
from jax.experimental import pallas as pl
from jax.experimental.pallas import tpu as pltpu

F32 = jnp.float32
BF16 = jnp.bfloat16

HEAD_DIM = 64
ATTN_GROUPS = ((128, 1), (512, 4), (2048, 16))
ATTN_HEADS_PER_GROUP = 4
ATTN_BLOCK = 128
GROUP_WIDTH = ATTN_HEADS_PER_GROUP * HEAD_DIM
ROPE_THETA = 500000.0
ROPE_DIM = HEAD_DIM // 4
DECAY_LORA = 64
ICLR_LORA = 64
GATE_LORA = 128
NORM_EPS = 1e-6
GN_EPS = 64e-5
NEG_INF = -1e30

LANES = 128
WKV_CHUNK = 64
VMEM_LIMIT = 56 * 1024 * 1024


def _dot(a, b):
    return jnp.dot(a, b, preferred_element_type=F32)


def _dot_nt(a, b):
    return lax.dot_general(a, b, (((1,), (1,)), ((), ())), preferred_element_type=F32)


def _dot_tn(a, b):
    return lax.dot_general(a, b, (((0,), (0,)), ((), ())), preferred_element_type=F32)


def _split2(x):
    hi = x.astype(BF16)
    lo = (x - hi.astype(F32)).astype(BF16)
    return hi, lo


def _split3(x):
    hi = x.astype(BF16)
    r = x - hi.astype(F32)
    mid = r.astype(BF16)
    lo = (r - mid.astype(F32)).astype(BF16)
    return hi, mid, lo


def _sigmoid(x):
    return 1.0 / (1.0 + jnp.exp(-x))


def _rms(t):
    return t * lax.rsqrt(jnp.mean(t * t, axis=-1, keepdims=True) + NORM_EPS)


def _const_spec(shape):
    return pl.BlockSpec(shape, lambda *_: (0,) * len(shape), pipeline_mode=pl.Buffered(1))


def _mod_kernel(c_ref, w_ref, b_ref, o_ref):
    c = c_ref[...]
    a = c * _sigmoid(c)
    ah, al = _split2(a)
    wh, wl = _split2(w_ref[...])
    o_ref[...] = _dot(ah, wh) + _dot(ah, wl) + _dot(al, wh) + b_ref[...]


def _mod(c_pad, ada_w, ada_b):
    rows, d = c_pad.shape
    n = ada_w.shape[1]
    tn = 768
    return pl.pallas_call(
        _mod_kernel,
        out_shape=jax.ShapeDtypeStruct((rows, n), F32),
        grid=(n // tn,),
        in_specs=[
            pl.BlockSpec((rows, d), lambda j: (0, 0)),
            pl.BlockSpec((d, tn), lambda j: (0, j)),
            pl.BlockSpec((1, tn), lambda j: (0, j)),
        ],
        out_specs=pl.BlockSpec((rows, tn), lambda j: (0, j)),
        name="mod",
    )(c_pad, ada_w, ada_b)


def _inproj_kernel(x_ref, pos_ref, sc_ref, sh_ref, g_ref, inv_ref, wa_ref, wr_ref, wg_ref, mu_ref,
                   a0_ref, a1_ref, a2_ref, rr_ref, rk_ref, rv_ref, rl_ref, gate_ref, carry_ref,
                   *, tiles_per_seq):
    i = pl.program_id(0)
    tm = x_ref.shape[0]
    d = x_ref.shape[1]
    h = _rms(x_ref[...]) * g_ref[...]
    hb = (h * (1.0 + sc_ref[...]) + sh_ref[...]).astype(BF16)

    ang = pos_ref[...] * inv_ref[...]
    cos = jnp.cos(ang)
    sin = jnp.sin(ang)
    lane = lax.broadcasted_iota(jnp.int32, (1, LANES), 1) % HEAD_DIM
    half = ROPE_DIM // 2
    cc = jnp.where(lane < ROPE_DIM, cos, 1.0)
    s_lo = jnp.where(lane < half, -sin, 0.0)
    s_hi = jnp.where((lane >= half) & (lane < ROPE_DIM), sin, 0.0)

    def rope(t):
        return t * cc + pltpu.roll(t, LANES - half, 1) * s_lo + pltpu.roll(t, half, 1) * s_hi

    a_refs = (a0_ref, a1_ref, a2_ref)
    aw = len(ATTN_GROUPS) * GROUP_WIDTH
    for which in range(3):
        t = _dot(hb, wa_ref[:, which * aw:(which + 1) * aw])
        for g in range(len(ATTN_GROUPS)):
            tg = t[:, g * GROUP_WIDTH:(g + 1) * GROUP_WIDTH]
            if which < 2:
                tg = jnp.concatenate([rope(tg[:, :LANES]), rope(tg[:, LANES:])], axis=1)
            if which == 0:
                tg = tg * (HEAD_DIM ** -0.5)
            a_refs[g][:, which * GROUP_WIDTH:(which + 1) * GROUP_WIDTH] = tg.astype(BF16)

    first = (i % tiles_per_seq) == 0
    row0 = lax.broadcasted_iota(jnp.int32, (tm, 1), 0) == 0
    col = 0
    for ref in (rr_ref, rk_ref, rv_ref, rl_ref):
        n = ref.shape[1]
        u = _dot(hb, wr_ref[:, col:col + n])
        carry = jnp.where(first, 0.0, carry_ref[:, col:col + n])
        prev = jnp.where(row0, carry, pltpu.roll(u, 1, 0))
        carry_ref[:, col:col + n] = u[tm - 1:tm, :]
        ref[...] = (u + (prev - u) * mu_ref[:, col:col + n]).astype(ref.dtype)
        col += n

    for j in range(2):
        gj = _dot(hb, wg_ref[:, j * d:(j + 1) * d])
        gate_ref[:, j * d:(j + 1) * d] = _sigmoid(gj).astype(BF16)


def _inproj(x2, posf, sc, sh, gain, inv_lane, wa, wr, wg, mu, *, seq, tm):
    t_rows, d = x2.shape
    tiles_per_seq = seq // tm
    aw = wa.shape[1] // 3
    shift_w = wr.shape[1]
    lora_w = shift_w - 3 * d
    row = lambda i: (i, 0)
    per_batch = lambda i: (i // tiles_per_seq, 0, 0)
    out_shape = (
        [jax.ShapeDtypeStruct((t_rows, aw), BF16)] * 3
        + [jax.ShapeDtypeStruct((t_rows, d), BF16)] * 3
        + [jax.ShapeDtypeStruct((t_rows, lora_w), F32), jax.ShapeDtypeStruct((t_rows, 2 * d), BF16)]
    )
    out_specs = (
        [pl.BlockSpec((tm, aw), row)] * 3
        + [pl.BlockSpec((tm, d), row)] * 3
        + [pl.BlockSpec((tm, lora_w), row), pl.BlockSpec((tm, 2 * d), row)]
    )
    return pl.pallas_call(
        functools.partial(_inproj_kernel, tiles_per_seq=tiles_per_seq),
        out_shape=out_shape,
        grid=(t_rows // tm,),
        in_specs=[
            pl.BlockSpec((tm, d), row),
            pl.BlockSpec((tm, 1), row),
            pl.BlockSpec((None, 1, d), per_batch),
            pl.BlockSpec((None, 1, d), per_batch),
            _const_spec((1, d)),
            _const_spec((1, LANES)),
            _const_spec(wa.shape),
            _const_spec(wr.shape),
            _const_spec(wg.shape),
            _const_spec((1, shift_w)),
        ],
        out_specs=out_specs,
        scratch_shapes=[pltpu.VMEM((1, shift_w), F32)],
        compiler_params=pltpu.CompilerParams(
            dimension_semantics=("arbitrary",), vmem_limit_bytes=VMEM_LIMIT),
        name="inproj",
    )(x2, posf, sc, sh, gain, inv_lane, wa, wr, wg, mu)


def _attn_kernel(*refs, blocks_per_subseq):
    n_groups = len(blocks_per_subseq)
    in_refs, out_refs = refs[:2 * n_groups], refs[2 * n_groups:]
    m = pl.program_id(1)
    blk = ATTN_BLOCK
    gw = GROUP_WIDTH
    row = lax.broadcasted_iota(jnp.int32, (blk, blk), 0)
    colk = lax.broadcasted_iota(jnp.int32, (blk, blk), 1)
    mask_cur = colk <= row
    mask_prev_band = colk >= row
    head_of_lane = lax.broadcasted_iota(jnp.int32, (1, gw), 1) // HEAD_DIM
    for g in range(n_groups):
        cur_ref, prev_ref = in_refs[2 * g], in_refs[2 * g + 1]
        o_ref, l_ref = out_refs[2 * g], out_refs[2 * g + 1]
        has_prev = (m % blocks_per_subseq[g]) > 0
        mask_prev = mask_prev_band & has_prev
        q = cur_ref[:, 0:gw]
        k = cur_ref[:, gw:2 * gw]
        v = cur_ref[:, 2 * gw:3 * gw]
        kp = prev_ref[:, gw:2 * gw]
        vp = prev_ref[:, 2 * gw:3 * gw]
        o_acc = jnp.zeros((blk, gw), F32)
        l_acc = jnp.zeros((blk, gw), F32)
        for hd in range(ATTN_HEADS_PER_GROUP):
            hm = head_of_lane == hd
            qh = jnp.where(hm, q, jnp.zeros_like(q))
            sc = jnp.where(mask_cur, _dot_nt(qh, k), NEG_INF)
            sp = jnp.where(mask_prev, _dot_nt(qh, kp), NEG_INF)
            mx = jnp.maximum(jnp.max(sc, axis=-1, keepdims=True), jnp.max(sp, axis=-1, keepdims=True))
            pc = jnp.exp(sc - mx)
            pp = jnp.exp(sp - mx)
            den = jnp.sum(pc, axis=-1, keepdims=True) + jnp.sum(pp, axis=-1, keepdims=True)
            pv = _dot(pc.astype(BF16), v) + _dot(pp.astype(BF16), vp)
            o_acc = jnp.where(hm, pv / den, o_acc)
            l_acc = jnp.where(hm, mx + jnp.log(den), l_acc)
        o_ref[...] = o_acc.astype(BF16)
        l_ref[...] = l_acc


def _attn(a_list, *, batch, seq):
    assert all(w // dil == ATTN_BLOCK for w, dil in ATTN_GROUPS)
    blk, gw = ATTN_BLOCK, GROUP_WIDTH
    n_blocks = seq // blk
    ins, in_specs, out_shape, out_specs, nbs = [], [], [], [], []
    for a, (_, dil) in zip(a_list, ATTN_GROUPS):
        assert seq % (dil * blk) == 0
        nb = n_blocks // dil
        nbs.append(nb)
        view = a.reshape(batch, seq // dil, dil * 3 * gw)
        cur = lambda b, m, nb=nb: (b, m % nb, m // nb)
        prev = lambda b, m, nb=nb: (b, jnp.maximum(m % nb - 1, 0), m // nb)
        ins += [view, view]
        in_specs += [pl.BlockSpec((None, blk, 3 * gw), cur), pl.BlockSpec((None, blk, 3 * gw), prev)]
        out_shape += [jax.ShapeDtypeStruct((batch, seq // dil, dil * gw), BF16),
                      jax.ShapeDtypeStruct((batch, seq // dil, dil * gw), F32)]
        out_specs += [pl.BlockSpec((None, blk, gw), cur)] * 2
    outs = pl.pallas_call(
        functools.partial(_attn_kernel, blocks_per_subseq=tuple(nbs)),
        out_shape=out_shape,
        grid=(batch, n_blocks),
        in_specs=in_specs,
        out_specs=out_specs,
        compiler_params=pltpu.CompilerParams(dimension_semantics=("arbitrary", "arbitrary")),
        name="attn",
    )(*ins)
    return [o.reshape(batch * seq, gw) for o in outs]


def _seg_sum(z, seg_ones):
    outs = []
    for j in range(z.shape[1] // LANES):
        hi, lo = _split2(z[:, j * LANES:(j + 1) * LANES])
        outs.append(_dot(hi, seg_ones) + _dot(lo, seg_ones))
    return jnp.concatenate(outs, axis=1)


def _stack_heads(x, lo_mask):
    z = jnp.zeros_like(x)
    return jnp.concatenate([jnp.where(lo_mask, x, z), jnp.where(lo_mask, z, x)], axis=0)


def _wkv_kernel(rr_ref, rk_ref, rv_ref, rl_ref, w0_ref, w2_ref, a0_ref, a2_ref, g2_ref, kk_ref, ka_ref,
                rkp_ref, gnw_ref, gnb_ref, o_ref,
                state_ref, lw_ref, r_ref, k_ref, v_ref, a_ref, b_ref, y_ref):
    tt, c = rr_ref.shape
    n_pairs = c // LANES
    ck = WKV_CHUNK

    @pl.when(pl.program_id(1) == 0)
    def _():
        state_ref[...] = jnp.zeros_like(state_ref)

    lane = lax.broadcasted_iota(jnp.int32, (LANES, LANES), 1)
    rowi = lax.broadcasted_iota(jnp.int32, (LANES, LANES), 0)
    seg_ones = jnp.where(lane // HEAD_DIM == rowi // HEAD_DIM, 1.0, 0.0).astype(BF16)

    r = rr_ref[...].astype(F32)
    k = rk_ref[...].astype(F32)
    v = rv_ref[...].astype(F32)
    lora = rl_ref[...]
    dwa = lora[:, 0:LANES]
    dg = lora[:, LANES:2 * LANES]
    zw = w0_ref[...] + _dot(jnp.tanh(dwa).astype(BF16), w2_ref[...])
    w_log = -(jnp.maximum(-zw, 0.0) + jnp.log(1.0 + jnp.exp(-jnp.abs(zw)))) - 0.5
    lw_ref[...] = -jnp.exp(w_log)
    a = _sigmoid(a0_ref[...] + _dot(dwa.astype(BF16), a2_ref[...]))
    g = _dot(_sigmoid(dg).astype(BF16), g2_ref[...])
    kk = k * kk_ref[...]
    kk = kk / jnp.maximum(jnp.sqrt(_seg_sum(kk * kk, seg_ones)), 1e-12)
    kmod = k * (1.0 + (a - 1.0) * ka_ref[...])
    bonus = _seg_sum(r * kmod * rkp_ref[...], seg_ones) * v
    r_ref[...] = r
    k_ref[...] = kmod
    v_ref[...] = v
    a_ref[...] = -kk
    b_ref[...] = kk * a

    tri = (lax.broadcasted_iota(jnp.int32, (ck, ck), 1)
           <= lax.broadcasted_iota(jnp.int32, (ck, ck), 0)).astype(BF16)
    t_idx = lax.broadcasted_iota(jnp.int32, (ck, LANES), 0)
    i_idx = lax.broadcasted_iota(jnp.int32, (ck, LANES), 1) % ck
    strict = i_idx < t_idx
    incl = i_idx <= t_idx
    lo_mask = lax.broadcasted_iota(jnp.int32, (1, LANES), 1) < HEAD_DIM
    eye2 = jnp.where(lane == rowi, 1.0, 0.0)

    def chunk_body(ci, carry):
        rows = pl.ds(pl.multiple_of(ci * ck, ck), ck)
        lw = lw_ref[rows, :]
        hi, mid, lo = _split3(lw)
        cum = _dot(tri, hi) + _dot(tri, mid) + _dot(tri, lo)
        e_in = jnp.exp(cum)
        e_ex = jnp.exp(cum - lw)
        e_neg = jnp.exp(-cum)
        at = (a_ref[rows, :] * e_ex).astype(BF16)
        rt = (r_ref[rows, :] * e_in).astype(BF16)
        bt = (b_ref[rows, :] * e_neg).astype(BF16)
        kt = (k_ref[rows, :] * e_neg).astype(BF16)
        vv = v_ref[rows, :].astype(BF16)
        gamma = e_in[ck - 1:ck, :]
        for p in range(n_pairs):
            ls = slice(p * LANES, (p + 1) * LANES)
            at_p, rt_p = at[:, ls], rt[:, ls]
            bs = _stack_heads(bt[:, ls], lo_mask)
            ks = _stack_heads(kt[:, ls], lo_mask)
            vw = _stack_heads(vv[:, ls], lo_mask)
            gm = _dot_nt(jnp.concatenate([at_p, rt_p], axis=0), jnp.concatenate([bs, ks], axis=0))
            a_ab = jnp.where(strict, gm[:ck, :LANES], 0.0)
            a_ak = jnp.where(strict, gm[:ck, LANES:], 0.0).astype(BF16)
            a_rb = jnp.where(incl, gm[ck:, :LANES], 0.0).astype(BF16)
            a_rk = jnp.where(incl, gm[ck:, LANES:], 0.0).astype(BF16)
            ap = _stack_heads(a_ab, lo_mask)
            tm_ = eye2 + ap
            pw = ap.astype(BF16)
            for _ in range(5):
                p2 = _dot(pw, pw)
                pw = p2.astype(BF16)
                tm_ = tm_ + _dot(tm_.astype(BF16), pw)
            t_cat = (tm_[:ck, :] + tm_[ck:, :]).astype(BF16)
            s_b = state_ref[p].astype(BF16)
            x = _dot_nt(at_p, s_b) + _dot(a_ak, vw)
            u = _dot(t_cat, _stack_heads(x, lo_mask).astype(BF16))
            uw = _stack_heads(u, lo_mask)
            uwb = uw.astype(BF16)
            y = _dot_nt(rt_p, s_b) + _dot(a_rb, uwb) + _dot(a_rk, vw)
            y_ref[rows, ls] = y
            ds = _dot_tn(jnp.concatenate([uwb, vw], axis=0), jnp.concatenate([bs, ks], axis=0))
            state_ref[p] = (state_ref[p] + ds) * gamma[:, ls]
        return carry

    lax.fori_loop(0, tt // ck, chunk_body, 0)

    y = y_ref[...]
    inv_n = 1.0 / HEAD_DIM
    mean = _seg_sum(y, seg_ones) * inv_n
    yc = y - mean
    var = _seg_sum(yc * yc, seg_ones) * inv_n
    yn = yc * lax.rsqrt(var + GN_EPS) * gnw_ref[...] + gnb_ref[...]
    o_ref[...] = ((yn + bonus) * g).astype(o_ref.dtype)


def _wkv(rr, rk, rv, rl, w0, w2p, a0, a2p, g2, k_k, k_a, r_k, gn_w, gn_b, *, batch, seq, tt):
    t_rows, c = rr.shape
    steps = seq // tt
    row = lambda b, t: (b * steps + t, 0)
    vec = _const_spec((1, c))
    return pl.pallas_call(
        _wkv_kernel,
        out_shape=jax.ShapeDtypeStruct((t_rows, c), BF16),
        grid=(batch, steps),
        in_specs=[
            pl.BlockSpec((tt, c), row), pl.BlockSpec((tt, c), row), pl.BlockSpec((tt, c), row),
            pl.BlockSpec((tt, rl.shape[1]), row),
            vec, _const_spec(w2p.shape), vec, _const_spec(a2p.shape), _const_spec(g2.shape),
            vec, vec, vec, vec, vec,
        ],
        out_specs=pl.BlockSpec((tt, c), row),
        scratch_shapes=[pltpu.VMEM((c // LANES, LANES, LANES), F32)] + [pltpu.VMEM((tt, c), F32)] * 7,
        compiler_params=pltpu.CompilerParams(
            dimension_semantics=("arbitrary", "arbitrary"), vmem_limit_bytes=VMEM_LIMIT),
        name="wkv",
    )(rr, rk, rv, rl, w0, w2p, a0, a2p, g2, k_k, k_a, r_k, gn_w, gn_b)


def _branch_kernel(o0_ref, l0_ref, o1_ref, l1_ref, o2_ref, l2_ref, gate_ref, orw_ref, x_ref,
                   wba_ref, wbr_ref, wo_ref, gt_ref, sc_ref, sh_ref, gpost_ref, gpre_ref,
                   x1_ref, h2_ref):
    d = x_ref.shape[1]
    l0, l1, l2 = l0_ref[...], l1_ref[...], l2_ref[...]
    mx = jnp.maximum(jnp.maximum(l0, l1), l2)
    e0, e1, e2 = jnp.exp(l0 - mx), jnp.exp(l1 - mx), jnp.exp(l2 - mx)
    o_attn = (e0 * o0_ref[...].astype(F32) + e1 * o1_ref[...].astype(F32)
              + e2 * o2_ref[...].astype(F32)) / (e0 + e1 + e2)
    ya = _dot(o_attn.astype(BF16), wba_ref[...])
    yr = _dot(orw_ref[...], wbr_ref[...])
    merged = gate_ref[:, 0:d].astype(F32) * ya + gate_ref[:, d:2 * d].astype(F32) * yr
    mix = _dot(merged.astype(BF16), wo_ref[...])
    x1 = x_ref[...] + gt_ref[...] * (_rms(mix) * gpost_ref[...])
    x1_ref[...] = x1
    h2 = _rms(x1) * gpre_ref[...]
    h2_ref[...] = (h2 * (1.0 + sc_ref[...]) + sh_ref[...]).astype(BF16)


def _branch(attn_outs, gates, o_rwkv, x2, wba, wbr, wo, gt, sc, sh, gpost, gpre, *, seq, tm):
    t_rows, d = x2.shape
    tiles_per_seq = seq // tm
    row = lambda i: (i, 0)
    per_batch = lambda i: (i // tiles_per_seq, 0, 0)
    gw = GROUP_WIDTH
    return pl.pallas_call(
        _branch_kernel,
        out_shape=[jax.ShapeDtypeStruct((t_rows, d), F32), jax.ShapeDtypeStruct((t_rows, d), BF16)],
        grid=(t_rows // tm,),
        in_specs=[pl.BlockSpec((tm, gw), row)] * 6 + [
            pl.BlockSpec((tm, 2 * d), row), pl.BlockSpec((tm, d), row), pl.BlockSpec((tm, d), row),
            _const_spec(wba.shape), _const_spec(wbr.shape), _const_spec(wo.shape),
            pl.BlockSpec((None, 1, d), per_batch), pl.BlockSpec((None, 1, d), per_batch),
            pl.BlockSpec((None, 1, d), per_batch),
            _const_spec((1, d)), _const_spec((1, d)),
        ],
        out_specs=[pl.BlockSpec((tm, d), row), pl.BlockSpec((tm, d), row)],
        compiler_params=pltpu.CompilerParams(
            dimension_semantics=("arbitrary",), vmem_limit_bytes=VMEM_LIMIT),
        name="branch",
    )(*attn_outs, gates, o_rwkv, x2, wba, wbr, wo, gt, sc, sh, gpost, gpre)


def _ffn_kernel(h_ref, x_ref, w1_ref, w2_ref, gt_ref, gpost_ref, o_ref, *, tf):
    h = h_ref[...]
    acc = jnp.zeros(x_ref.shape, F32)
    for j in range(w1_ref.shape[1] // tf):
        a = jnp.maximum(_dot(h, w1_ref[:, j * tf:(j + 1) * tf]), 0.0)
        acc = acc + _dot((a * a).astype(BF16), w2_ref[j * tf:(j + 1) * tf, :])
    o_ref[...] = x_ref[...] + gt_ref[...] * (_rms(acc) * gpost_ref[...])


def _ffn(h2, x1, w1, w2, gt, gpost, *, seq, tm, tf):
    t_rows, d = x1.shape
    tiles_per_seq = seq // tm
    row = lambda i: (i, 0)
    per_batch = lambda i: (i // tiles_per_seq, 0, 0)
    return pl.pallas_call(
        functools.partial(_ffn_kernel, tf=tf),
        out_shape=jax.ShapeDtypeStruct((t_rows, d), F32),
        grid=(t_rows // tm,),
        in_specs=[
            pl.BlockSpec((tm, d), row), pl.BlockSpec((tm, d), row),
            _const_spec(w1.shape), _const_spec(w2.shape),
            pl.BlockSpec((None, 1, d), per_batch), _const_spec((1, d)),
        ],
        out_specs=pl.BlockSpec((tm, d), row),
        compiler_params=pltpu.CompilerParams(
            dimension_semantics=("arbitrary",), vmem_limit_bytes=VMEM_LIMIT),
        name="ffn",
    )(h2, x1, w1, w2, gt, gpost)


def kernel(x, c, positions, ada_w, ada_b, norm_mix_pre, norm_mix_post, norm_ffn_pre, norm_ffn_post, w_in, shift_mu, decay_w0, decay_w2, iclr_a0, iclr_a2, gate_g2, k_k, k_a, r_k, gn_w, gn_b, w_branch, w_out, w_ff1, w_ff2):
    batch, seq, d = x.shape
    depth = ada_w.shape[0]
    t_rows = batch * seq
    n_groups = len(ATTN_GROUPS)
    attn_w = n_groups * GROUP_WIDTH
    attn_out_w = GROUP_WIDTH
    shift_w = shift_mu.shape[1]
    assert shift_w == 3 * d + DECAY_LORA + ICLR_LORA + GATE_LORA
    assert DECAY_LORA + ICLR_LORA == LANES and GATE_LORA == LANES
    tm = 512

    x2 = x.reshape(t_rows, d)
    posf = positions.astype(F32).reshape(t_rows, 1)
    half = ROPE_DIM // 2
    inv = ROPE_THETA ** (-jnp.arange(half, dtype=F32) * 2.0 / ROPE_DIM)
    inv_lane = jnp.tile(inv, LANES // half).reshape(1, LANES)
    c_pad = jnp.pad(c, ((0, 8 - batch), (0, 0)))
    vec = lambda p: p.reshape(1, -1)

    for l in range(depth):
        mod = _mod(c_pad, ada_w[l], ada_b[l].reshape(1, -1))[:batch]
        sh1, sc1, gt1, sh2, sc2, gt2 = [m.reshape(batch, 1, d) for m in jnp.split(mod, 6, axis=-1)]

        w = w_in[l].astype(BF16)
        wa = w[:, :3 * attn_w]
        wr = w[:, 3 * attn_w:3 * attn_w + shift_w]
        wg = w[:, 3 * attn_w + shift_w:]
        a0, a1, a2, rr, rk, rv, rl, gates = _inproj(
            x2, posf, sc1, sh1, vec(norm_mix_pre[l]), inv_lane, wa, wr, wg, vec(shift_mu[l]),
            seq=seq, tm=tm)

        attn_outs = _attn([a0, a1, a2], batch=batch, seq=seq)

        zeros = jnp.zeros((DECAY_LORA, d), BF16)
        w2p = jnp.concatenate([decay_w2[l].astype(BF16), zeros], axis=0)
        a2p = jnp.concatenate([zeros, iclr_a2[l].astype(BF16)], axis=0)
        o_rwkv = _wkv(rr, rk, rv, rl, vec(decay_w0[l]), w2p, vec(iclr_a0[l]), a2p,
                      gate_g2[l].astype(BF16), vec(k_k[l]), vec(k_a[l]), vec(r_k[l]),
                      vec(gn_w[l]), vec(gn_b[l]), batch=batch, seq=seq, tt=512)

        wb = w_branch[l].astype(BF16)
        x1, h2 = _branch(attn_outs, gates, o_rwkv, x2, wb[:attn_out_w], wb[attn_out_w:],
                         w_out[l].astype(BF16), gt1, sc2, sh2, vec(norm_mix_post[l]),
                         vec(norm_ffn_pre[l]), seq=seq, tm=tm)
        x2 = _ffn(h2, x1, w_ff1[l].astype(BF16), w_ff2[l].astype(BF16), gt2, vec(norm_ffn_post[l]),
                  seq=seq, tm=tm, tf=512)
    return x2.reshape(batch, seq, d)
```

```python
import functools

import jax
import jax.numpy as jnp
from jax import lax
from jax.experimental import pallas as pl
from jax.experimental.pallas import tpu as pltpu

F32 = jnp.float32
BF16 = jnp.bfloat16

HEAD_DIM = 64
ATTN_GROUPS = ((128, 1), (512, 4), (2048, 16))
ATTN_HEADS_PER_GROUP = 4
ATTN_BLOCK = 128
GROUP_WIDTH = ATTN_HEADS_PER_GROUP * HEAD_DIM
ROPE_THETA = 500000.0
ROPE_DIM = HEAD_DIM // 4
DECAY_LORA = 64
ICLR_LORA = 64
GATE_LORA = 128
NORM_EPS = 1e-6
GN_EPS = 64e-5
NEG_INF = -1e30

LANES = 128
WKV_CHUNK = 64
VMEM_LIMIT = 56 * 1024 * 1024


def _dot(a, b):
    return jnp.dot(a, b, preferred_element_type=F32)


def _dot_nt(a, b):
    return lax.dot_general(a, b, (((1,), (1,)), ((), ())), preferred_element_type=F32)


def _split2(x):
    hi = x.astype(BF16)
    lo = (x - hi.astype(F32)).astype(BF16)
    return hi, lo


def _split3(x):
    hi = x.astype(BF16)
    r = x - hi.astype(F32)
    mid = r.astype(BF16)
    lo = (r - mid.astype(F32)).astype(BF16)
    return hi, mid, lo


def _sigmoid(x):
    return 1.0 / (1.0 + jnp.exp(-x))


def _rms(t):
    return t * lax.rsqrt(jnp.mean(t * t, axis=-1, keepdims=True) + NORM_EPS)


def _const_spec(shape):
    return pl.BlockSpec(shape, lambda *_: (0,) * len(shape), pipeline_mode=pl.Buffered(1))


def _mod_kernel(c_ref, w_ref, b_ref, o_ref):
    c = c_ref[...]
    a = c * _sigmoid(c)
    ah, al = _split2(a)
    wh, wl = _split2(w_ref[...])
    o_ref[...] = _dot(ah, wh) + _dot(ah, wl) + _dot(al, wh) + b_ref[...]


def _mod(c_pad, ada_w, ada_b):
    rows, d = c_pad.shape
    n = ada_w.shape[1]
    tn = 768
    return pl.pallas_call(
        _mod_kernel,
        out_shape=jax.ShapeDtypeStruct((rows, n), F32),
        grid=(n // tn,),
        in_specs=[
            pl.BlockSpec((rows, d), lambda j: (0, 0)),
            pl.BlockSpec((d, tn), lambda j: (0, j)),
            pl.BlockSpec((1, tn), lambda j: (0, j)),
        ],
        out_specs=pl.BlockSpec((rows, tn), lambda j: (0, j)),
        name="mod",
    )(c_pad, ada_w, ada_b)


def _inproj_kernel(x_ref, pos_ref, sc_ref, sh_ref, g_ref, inv_ref, wa_ref, wr_ref, wg_ref, mu_ref,
                   a0_ref, a1_ref, a2_ref, rr_ref, rk_ref, rv_ref, rl_ref, gate_ref, carry_ref,
                   *, tiles_per_seq):
    i = pl.program_id(0)
    tm = x_ref.shape[0]
    d = x_ref.shape[1]
    h = _rms(x_ref[...]) * g_ref[...]
    hb = (h * (1.0 + sc_ref[...]) + sh_ref[...]).astype(BF16)

    ang = pos_ref[...] * inv_ref[...]
    cos = jnp.cos(ang)
    sin = jnp.sin(ang)
    lane = lax.broadcasted_iota(jnp.int32, (1, LANES), 1) % HEAD_DIM
    half = ROPE_DIM // 2
    cc = jnp.where(lane < ROPE_DIM, cos, 1.0)
    s_lo = jnp.where(lane < half, -sin, 0.0)
    s_hi = jnp.where((lane >= half) & (lane < ROPE_DIM), sin, 0.0)

    def rope(t):
        return t * cc + pltpu.roll(t, LANES - half, 1) * s_lo + pltpu.roll(t, half, 1) * s_hi

    a_refs = (a0_ref, a1_ref, a2_ref)
    aw = len(ATTN_GROUPS) * GROUP_WIDTH
    for which in range(3):
        t = _dot(hb, wa_ref[:, which * aw:(which + 1) * aw])
        for g in range(len(ATTN_GROUPS)):
            tg = t[:, g * GROUP_WIDTH:(g + 1) * GROUP_WIDTH]
            if which < 2:
                tg = jnp.concatenate([rope(tg[:, :LANES]), rope(tg[:, LANES:])], axis=1)
            if which == 0:
                tg = tg * (HEAD_DIM ** -0.5)
            a_refs[g][:, which * GROUP_WIDTH:(which + 1) * GROUP_WIDTH] = tg.astype(BF16)

    first = (i % tiles_per_seq) == 0
    row0 = lax.broadcasted_iota(jnp.int32, (tm, 1), 0) == 0
    col = 0
    for ref in (rr_ref, rk_ref, rv_ref, rl_ref):
        n = ref.shape[1]
        u = _dot(hb, wr_ref[:, col:col + n])
        carry = jnp.where(first, 0.0, carry_ref[:, col:col + n])
        prev = jnp.where(row0, carry, pltpu.roll(u, 1, 0))
        carry_ref[:, col:col + n] = u[tm - 1:tm, :]
        ref[...] = (u + (prev - u) * mu_ref[:, col:col + n]).astype(ref.dtype)
        col += n

    for j in range(2):
        gj = _dot(hb, wg_ref[:, j * d:(j + 1) * d])
        gate_ref[:, j * d:(j + 1) * d] = _sigmoid(gj).astype(BF16)


def _inproj(x2, posf, sc, sh, gain, inv_lane, wa, wr, wg, mu, *, seq, tm):
    t_rows, d = x2.shape
    tiles_per_seq = seq // tm
    aw = wa.shape[1] // 3
    shift_w = wr.shape[1]
    lora_w = shift_w - 3 * d
    row = lambda i: (i, 0)
    per_batch = lambda i: (i // tiles_per_seq, 0, 0)
    out_shape = (
        [jax.ShapeDtypeStruct((t_rows, aw), BF16)] * 3
        + [jax.ShapeDtypeStruct((t_rows, d), BF16)] * 3
        + [jax.ShapeDtypeStruct((t_rows, lora_w), F32), jax.ShapeDtypeStruct((t_rows, 2 * d), BF16)]
    )
    out_specs = (
        [pl.BlockSpec((tm, aw), row)] * 3
        + [pl.BlockSpec((tm, d), row)] * 3
        + [pl.BlockSpec((tm, lora_w), row), pl.BlockSpec((tm, 2 * d), row)]
    )
    return pl.pallas_call(
        functools.partial(_inproj_kernel, tiles_per_seq=tiles_per_seq),
        out_shape=out_shape,
        grid=(t_rows // tm,),
        in_specs=[
            pl.BlockSpec((tm, d), row),
            pl.BlockSpec((tm, 1), row),
            pl.BlockSpec((None, 1, d), per_batch),
            pl.BlockSpec((None, 1, d), per_batch),
            _const_spec((1, d)),
            _const_spec((1, LANES)),
            _const_spec(wa.shape),
            _const_spec(wr.shape),
            _const_spec(wg.shape),
            _const_spec((1, shift_w)),
        ],
        out_specs=out_specs,
        scratch_shapes=[pltpu.VMEM((1, shift_w), F32)],
        compiler_params=pltpu.CompilerParams(
            dimension_semantics=("arbitrary",), vmem_limit_bytes=VMEM_LIMIT),
        name="inproj",
    )(x2, posf, sc, sh, gain, inv_lane, wa, wr, wg, mu)


def _attn_kernel(*refs, blocks_per_subseq):
    n_groups = len(blocks_per_subseq)
    in_refs, out_refs = refs[:2 * n_groups], refs[2 * n_groups:]
    m = pl.program_id(1)
    blk = ATTN_BLOCK
    gw = GROUP_WIDTH
    row = lax.broadcasted_iota(jnp.int32, (blk, blk), 0)
    colk = lax.broadcasted_iota(jnp.int32, (blk, blk), 1)
    mask_cur = colk <= row
    mask_prev_band = colk >= row
    head_of_lane = lax.broadcasted_iota(jnp.int32, (1, gw), 1) // HEAD_DIM
    for g in range(n_groups):
        cur_ref, prev_ref = in_refs[2 * g], in_refs[2 * g + 1]
        o_ref, l_ref = out_refs[2 * g], out_refs[2 * g + 1]
        has_prev = (m % blocks_per_subseq[g]) > 0
        mask_prev = mask_prev_band & has_prev
        q = cur_ref[:, 0:gw]
        k = cur_ref[:, gw:2 * gw]
        v = cur_ref[:, 2 * gw:3 * gw]
        kp = prev_ref[:, gw:2 * gw]
        vp = prev_ref[:, 2 * gw:3 * gw]
        o_acc = jnp.zeros((blk, gw), F32)
        l_acc = jnp.zeros((blk, gw), F32)
        for hd in range(ATTN_HEADS_PER_GROUP):
            hm = head_of_lane == hd
            qh = jnp.where(hm, q, jnp.zeros_like(q))
            sc = jnp.where(mask_cur, _dot_nt(qh, k), NEG_INF)
            sp = jnp.where(mask_prev, _dot_nt(qh, kp), NEG_INF)
            mx = jnp.maximum(jnp.max(sc, axis=-1, keepdims=True), jnp.max(sp, axis=-1, keepdims=True))
            pc = jnp.exp(sc - mx)
            pp = jnp.exp(sp - mx)
            den = jnp.sum(pc, axis=-1, keepdims=True) + jnp.sum(pp, axis=-1, keepdims=True)
            pv = _dot(pc.astype(BF16), v) + _dot(pp.astype(BF16), vp)
            o_acc = jnp.where(hm, pv / den, o_acc)
            l_acc = jnp.where(hm, mx + jnp.log(den), l_acc)
        o_ref[...] = o_acc.astype(BF16)
        l_ref[...] = l_acc


def _attn(a_list, *, batch, seq):
    assert all(w // dil == ATTN_BLOCK for w, dil in ATTN_GROUPS)
    blk, gw = ATTN_BLOCK, GROUP_WIDTH
    n_blocks = seq // blk
    ins, in_specs, out_shape, out_specs, nbs = [], [], [], [], []
    for a, (_, dil) in zip(a_list, ATTN_GROUPS):
        assert seq % (dil * blk) == 0
        nb = n_blocks // dil
        nbs.append(nb)
        view = a.reshape(batch, seq // dil, dil * 3 * gw)
        cur = lambda b, m, nb=nb: (b, m % nb, m // nb)
        prev = lambda b, m, nb=nb: (b, jnp.maximum(m % nb - 1, 0), m // nb)
        ins += [view, view]
        in_specs += [pl.BlockSpec((None, blk, 3 * gw), cur), pl.BlockSpec((None, blk, 3 * gw), prev)]
        out_shape += [jax.ShapeDtypeStruct((batch, seq // dil, dil * gw), BF16),
                      jax.ShapeDtypeStruct((batch, seq // dil, dil * gw), F32)]
        out_specs += [pl.BlockSpec((None, blk, gw), cur)] * 2
    outs = pl.pallas_call(
        functools.partial(_attn_kernel, blocks_per_subseq=tuple(nbs)),
        out_shape=out_shape,
        grid=(batch, n_blocks),
        in_specs=in_specs,
        out_specs=out_specs,
        compiler_params=pltpu.CompilerParams(dimension_semantics=("arbitrary", "arbitrary")),
        name="attn",
    )(*ins)
    return [o.reshape(batch * seq, gw) for o in outs]


def _seg_sum(z, seg_ones):
    outs = []
    for j in range(z.shape[1] // LANES):
        hi, lo = _split2(z[:, j * LANES:(j + 1) * LANES])
        outs.append(_dot(hi, seg_ones) + _dot(lo, seg_ones))
    return jnp.concatenate(outs, axis=1)


def _wkv_kernel(rr_ref, rk_ref, rv_ref, rl_ref, w0_ref, w2_ref, a0_ref, a2_ref, g2_ref, kk_ref, ka_ref,
                rkp_ref, gnw_ref, gnb_ref, o_ref,
                state_ref, lw_ref, r_ref, k_ref, v_ref, a_ref, b_ref, y_ref):
    tt, c = rr_ref.shape
    n_pairs = c // LANES
    ck = WKV_CHUNK

    @pl.when(pl.program_id(1) == 0)
    def _():
        state_ref[...] = jnp.zeros_like(state_ref)

    lane = lax.broadcasted_iota(jnp.int32, (LANES, LANES), 1)
    rowi = lax.broadcasted_iota(jnp.int32, (LANES, LANES), 0)
    seg_ones = jnp.where(lane // HEAD_DIM == rowi // HEAD_DIM, 1.0, 0.0).astype(BF16)

    r = rr_ref[...].astype(F32)
    k = rk_ref[...].astype(F32)
    v = rv_ref[...].astype(F32)
    lora = rl_ref[...]
    dwa = lora[:, 0:LANES]
    dg = lora[:, LANES:2 * LANES]
    zw = w0_ref[...] + _dot(jnp.tanh(dwa).astype(BF16), w2_ref[...])
    w_log = -(jnp.maximum(-zw, 0.0) + jnp.log(1.0 + jnp.exp(-jnp.abs(zw)))) - 0.5
    lw_ref[...] = -jnp.exp(w_log)
    a = _sigmoid(a0_ref[...] + _dot(dwa.astype(BF16), a2_ref[...]))
    g = _dot(_sigmoid(dg).astype(BF16), g2_ref[...])
    kk = k * kk_ref[...]
    kk = kk / jnp.maximum(jnp.sqrt(_seg_sum(kk * kk, seg_ones)), 1e-12)
    kmod = k * (1.0 + (a - 1.0) * ka_ref[...])
    bonus = _seg_sum(r * kmod * rkp_ref[...], seg_ones) * v
    r_ref[...] = r
    k_ref[...] = kmod
    v_ref[...] = v
    a_ref[...] = -kk
    b_ref[...] = kk * a

    tri = (lax.broadcasted_iota(jnp.int32, (ck, ck), 1)
           <= lax.broadcasted_iota(jnp.int32, (ck, ck), 0)).astype(BF16)
    t_idx = lax.broadcasted_iota(jnp.int32, (ck, LANES), 0)
    i_idx = lax.broadcasted_iota(jnp.int32, (ck, LANES), 1) % ck
    strict = i_idx < t_idx
    incl = i_idx <= t_idx
    lo_mask = lax.broadcasted_iota(jnp.int32, (1, LANES), 1) < HEAD_DIM
    eye2 = jnp.where(lane == rowi, 1.0, 0.0)
    bk_row_head = lax.broadcasted_iota(jnp.int32, (LANES, 2 * LANES), 0) // HEAD_DIM
    bk_col_head = (lax.broadcasted_iota(jnp.int32, (LANES, 2 * LANES), 1) // ck) % 2
    bk_mask = bk_row_head == bk_col_head

    def pairs(t):
        return jnp.stack([t[:, p * LANES:(p + 1) * LANES] for p in range(n_pairs)])

    def stack_heads(t):
        z = jnp.zeros_like(t)
        return jnp.concatenate([jnp.where(lo_mask, t, z), jnp.where(lo_mask, z, t)], axis=1)

    def bmm(a_, b_):
        return jnp.einsum('pmk,pkn->pmn', a_, b_, preferred_element_type=F32)

    def chunk_body(ci, carry):
        rows = pl.ds(pl.multiple_of(ci * ck, ck), ck)
        lw = lw_ref[rows, :]
        hi, mid, lo = _split3(lw)
        cum = _dot(tri, hi) + _dot(tri, mid) + _dot(tri, lo)
        e_in = jnp.exp(cum)
        e_ex = jnp.exp(cum - lw)
        e_neg = jnp.exp(-cum)
        ar = jnp.concatenate([pairs(a_ref[rows, :] * e_ex), pairs(r_ref[rows, :] * e_in)],
                             axis=1).astype(BF16)
        bt = pairs(b_ref[rows, :] * e_neg)
        kt = pairs(k_ref[rows, :] * e_neg)
        bk_t = jnp.swapaxes(jnp.concatenate([bt, bt, kt, kt], axis=1), 1, 2)
        bk_t = jnp.where(bk_mask, bk_t, 0.0).astype(BF16)
        vw = stack_heads(pairs(v_ref[rows, :])).astype(BF16)
        gcol = jnp.swapaxes(pairs(e_in[ck - 8:ck, :]), 1, 2)[:, :, 7:8]

        gm = bmm(ar, bk_t)
        a_ab = jnp.where(strict, gm[:, :ck, :LANES], 0.0)
        a_ak = jnp.where(strict, gm[:, :ck, LANES:], 0.0).astype(BF16)
        a_rb = jnp.where(incl, gm[:, ck:, :LANES], 0.0).astype(BF16)
        a_rk = jnp.where(incl, gm[:, ck:, LANES:], 0.0).astype(BF16)
        ap = stack_heads(a_ab)
        tm_ = eye2 + ap
        pw = ap.astype(BF16)
        for _ in range(5):
            pw = bmm(pw, pw).astype(BF16)
            tm_ = tm_ + bmm(tm_.astype(BF16), pw)
        t_cat = (tm_[:, :ck, :] + tm_[:, ck:, :]).astype(BF16)

        s = state_ref[...]
        sx = bmm(ar, s.astype(BF16))
        x = sx[:, :ck, :] + bmm(a_ak, vw)
        u = bmm(t_cat, stack_heads(x).astype(BF16))
        uw = stack_heads(u).astype(BF16)
        y = sx[:, ck:, :] + bmm(a_rb, uw) + bmm(a_rk, vw)
        y_ref[rows, :] = jnp.concatenate([y[p] for p in range(n_pairs)], axis=1)
        ds = bmm(bk_t, jnp.concatenate([uw, vw], axis=1))
        state_ref[...] = (s + ds) * gcol
        return carry

    lax.fori_loop(0, tt // ck, chunk_body, 0)

    y = y_ref[...]
    inv_n = 1.0 / HEAD_DIM
    mean = _seg_sum(y, seg_ones) * inv_n
    yc = y - mean
    var = _seg_sum(yc * yc, seg_ones) * inv_n
    yn = yc * lax.rsqrt(var + GN_EPS) * gnw_ref[...] + gnb_ref[...]
    o_ref[...] = ((yn + bonus) * g).astype(o_ref.dtype)


def _wkv(rr, rk, rv, rl, w0, w2p, a0, a2p, g2, k_k, k_a, r_k, gn_w, gn_b, *, batch, seq, tt):
    t_rows, c = rr.shape
    steps = seq // tt
    row = lambda b, t: (b * steps + t, 0)
    vec = _const_spec((1, c))
    return pl.pallas_call(
        _wkv_kernel,
        out_shape=jax.ShapeDtypeStruct((t_rows, c), BF16),
        grid=(batch, steps),
        in_specs=[
            pl.BlockSpec((tt, c), row), pl.BlockSpec((tt, c), row), pl.BlockSpec((tt, c), row),
            pl.BlockSpec((tt, rl.shape[1]), row),
            vec, _const_spec(w2p.shape), vec, _const_spec(a2p.shape), _const_spec(g2.shape),
            vec, vec, vec, vec, vec,
        ],
        out_specs=pl.BlockSpec((tt, c), row),
        scratch_shapes=[pltpu.VMEM((c // LANES, LANES, LANES), F32)] + [pltpu.VMEM((tt, c), F32)] * 7,
        compiler_params=pltpu.CompilerParams(
            dimension_semantics=("arbitrary", "arbitrary"), vmem_limit_bytes=VMEM_LIMIT),
        name="wkv",
    )(rr, rk, rv, rl, w0, w2p, a0, a2p, g2, k_k, k_a, r_k, gn_w, gn_b)


def _branch_kernel(o0_ref, l0_ref, o1_ref, l1_ref, o2_ref, l2_ref, gate_ref, orw_ref, x_ref,
                   wba_ref, wbr_ref, wo_ref, gt_ref, sc_ref, sh_ref, gpost_ref, gpre_ref,
                   x1_ref, h2_ref):
    d = x_ref.shape[1]
    l0, l1, l2 = l0_ref[...], l1_ref[...], l2_ref[...]
    mx = jnp.maximum(jnp.maximum(l0, l1), l2)
    e0, e1, e2 = jnp.exp(l0 - mx), jnp.exp(l1 - mx), jnp.exp(l2 - mx)
    o_attn = (e0 * o0_ref[...].astype(F32) + e1 * o1_ref[...].astype(F32)
              + e2 * o2_ref[...].astype(F32)) / (e0 + e1 + e2)
    ya = _dot(o_attn.astype(BF16), wba_ref[...])
    yr = _dot(orw_ref[...], wbr_ref[...])
    merged = gate_ref[:, 0:d].astype(F32) * ya + gate_ref[:, d:2 * d].astype(F32) * yr
    mix = _dot(merged.astype(BF16), wo_ref[...])
    x1 = x_ref[...] + gt_ref[...] * (_rms(mix) * gpost_ref[...])
    x1_ref[...] = x1
    h2 = _rms(x1) * gpre_ref[...]
    h2_ref[...] = (h2 * (1.0 + sc_ref[...]) + sh_ref[...]).astype(BF16)


def _branch(attn_outs, gates, o_rwkv, x2, wba, wbr, wo, gt, sc, sh, gpost, gpre, *, seq, tm):
    t_rows, d = x2.shape
    tiles_per_seq = seq // tm
    row = lambda i: (i, 0)
    per_batch = lambda i: (i // tiles_per_seq, 0, 0)
    gw = GROUP_WIDTH
    return pl.pallas_call(
        _branch_kernel,
        out_shape=[jax.ShapeDtypeStruct((t_rows, d), F32), jax.ShapeDtypeStruct((t_rows, d), BF16)],
        grid=(t_rows // tm,),
        in_specs=[pl.BlockSpec((tm, gw), row)] * 6 + [
            pl.BlockSpec((tm, 2 * d), row), pl.BlockSpec((tm, d), row), pl.BlockSpec((tm, d), row),
            _const_spec(wba.shape), _const_spec(wbr.shape), _const_spec(wo.shape),
            pl.BlockSpec((None, 1, d), per_batch), pl.BlockSpec((None, 1, d), per_batch),
            pl.BlockSpec((None, 1, d), per_batch),
            _const_spec((1, d)), _const_spec((1, d)),
        ],
        out_specs=[pl.BlockSpec((tm, d), row), pl.BlockSpec((tm, d), row)],
        compiler_params=pltpu.CompilerParams(
            dimension_semantics=("arbitrary",), vmem_limit_bytes=VMEM_LIMIT),
        name="branch",
    )(*attn_outs, gates, o_rwkv, x2, wba, wbr, wo, gt, sc, sh, gpost, gpre)


def _ffn_kernel(h_ref, x_ref, w1_ref, w2_ref, gt_ref, gpost_ref, o_ref, *, tf):
    h = h_ref[...]
    acc = jnp.zeros(x_ref.shape, F32)
    for j in range(w1_ref.shape[1] // tf):
        a = jnp.maximum(_dot(h, w1_ref[:, j * tf:(j + 1) * tf]), 0.0)
        acc = acc + _dot((a * a).astype(BF16), w2_ref[j * tf:(j + 1) * tf, :])
    o_ref[...] = x_ref[...] + gt_ref[...] * (_rms(acc) * gpost_ref[...])


def _ffn(h2, x1, w1, w2, gt, gpost, *, seq, tm, tf):
    t_rows, d = x1.shape
    tiles_per_seq = seq // tm
    row = lambda i: (i, 0)
    per_batch = lambda i: (i // tiles_per_seq, 0, 0)
    return pl.pallas_call(
        functools.partial(_ffn_kernel, tf=tf),
        out_shape=jax.ShapeDtypeStruct((t_rows, d), F32),
        grid=(t_rows // tm,),
        in_specs=[
            pl.BlockSpec((tm, d), row), pl.BlockSpec((tm, d), row),
            _const_spec(w1.shape), _const_spec(w2.shape),
            pl.BlockSpec((None, 1, d), per_batch), _const_spec((1, d)),
        ],
        out_specs=pl.BlockSpec((tm, d), row),
        compiler_params=pltpu.CompilerParams(
            dimension_semantics=("arbitrary",), vmem_limit_bytes=VMEM_LIMIT),
        name="ffn",
    )(h2, x1, w1, w2, gt, gpost)


def kernel(x, c, positions, ada_w, ada_b, norm_mix_pre, norm_mix_post, norm_ffn_pre, norm_ffn_post, w_in, shift_mu, decay_w0, decay_w2, iclr_a0, iclr_a2, gate_g2, k_k, k_a, r_k, gn_w, gn_b, w_branch, w_out, w_ff1, w_ff2):
    batch, seq, d = x.shape
    depth = ada_w.shape[0]
    t_rows = batch * seq
    n_groups = len(ATTN_GROUPS)
    attn_w = n_groups * GROUP_WIDTH
    attn_out_w = GROUP_WIDTH
    shift_w = shift_mu.shape[1]
    assert shift_w == 3 * d + DECAY_LORA + ICLR_LORA + GATE_LORA
    assert DECAY_LORA + ICLR_LORA == LANES and GATE_LORA == LANES
    tm = 512

    x2 = x.reshape(t_rows, d)
    posf = positions.astype(F32).reshape(t_rows, 1)
    half = ROPE_DIM // 2
    inv = ROPE_THETA ** (-jnp.arange(half, dtype=F32) * 2.0 / ROPE_DIM)
    inv_lane = jnp.tile(inv, LANES // half).reshape(1, LANES)
    c_pad = jnp.pad(c, ((0, 8 - batch), (0, 0)))
    vec = lambda p: p.reshape(1, -1)

    for l in range(depth):
        mod = _mod(c_pad, ada_w[l], ada_b[l].reshape(1, -1))[:batch]
        sh1, sc1, gt1, sh2, sc2, gt2 = [m.reshape(batch, 1, d) for m in jnp.split(mod, 6, axis=-1)]

        w = w_in[l].astype(BF16)
        wa = w[:, :3 * attn_w]
        wr = w[:, 3 * attn_w:3 * attn_w + shift_w]
        wg = w[:, 3 * attn_w + shift_w:]
        a0, a1, a2, rr, rk, rv, rl, gates = _inproj(
            x2, posf, sc1, sh1, vec(norm_mix_pre[l]), inv_lane, wa, wr, wg, vec(shift_mu[l]),
            seq=seq, tm=tm)

        attn_outs = _attn([a0, a1, a2], batch=batch, seq=seq)

        zeros = jnp.zeros((DECAY_LORA, d), BF16)
        w2p = jnp.concatenate([decay_w2[l].astype(BF16), zeros], axis=0)
        a2p = jnp.concatenate([zeros, iclr_a2[l].astype(BF16)], axis=0)
        o_rwkv = _wkv(rr, rk, rv, rl, vec(decay_w0[l]), w2p, vec(iclr_a0[l]), a2p,
                      gate_g2[l].astype(BF16), vec(k_k[l]), vec(k_a[l]), vec(r_k[l]),
                      vec(gn_w[l]), vec(gn_b[l]), batch=batch, seq=seq, tt=512)

        wb = w_branch[l].astype(BF16)
        x1, h2 = _branch(attn_outs, gates, o_rwkv, x2, wb[:attn_out_w], wb[attn_out_w:],
                         w_out[l].astype(BF16), gt1, sc2, sh2, vec(norm_mix_post[l]),
                         vec(norm_ffn_pre[l]), seq=seq, tm=tm)
        x2 = _ffn(h2, x1, w_ff1[l].astype(BF16), w_ff2[l].astype(BF16), gt2, vec(norm_ffn_post[l]),
                  seq=seq, tm=tm, tf=512)
    return x2.reshape(batch, seq, d)
```

```python
import functools

import jax
import jax.numpy as jnp
from jax import lax
from jax.experimental import pallas as pl
from jax.experimental.pallas import tpu as pltpu

F32 = jnp.float32
BF16 = jnp.bfloat16

HEAD_DIM = 64
ATTN_GROUPS = ((128, 1), (512, 4), (2048, 16))
ATTN_HEADS_PER_GROUP = 4
ATTN_BLOCK = 128
GROUP_WIDTH = ATTN_HEADS_PER_GROUP * HEAD_DIM
ROPE_THETA = 500000.0
ROPE_DIM = HEAD_DIM // 4
DECAY_LORA = 64
ICLR_LORA = 64
GATE_LORA = 128
NORM_EPS = 1e-6
GN_EPS = 64e-5
NEG_INF = -1e30

LANES = 128
WKV_CHUNK = 64
VMEM_LIMIT = 56 * 1024 * 1024


def _dot(a, b):
    return jnp.dot(a, b, preferred_element_type=F32)


def _dot_nt(a, b):
    return lax.dot_general(a, b, (((1,), (1,)), ((), ())), preferred_element_type=F32)


def _split2(x):
    hi = x.astype(BF16)
    lo = (x - hi.astype(F32)).astype(BF16)
    return hi, lo


def _sigmoid(x):
    return 1.0 / (1.0 + jnp.exp(-x))


def _rms(t):
    return t * lax.rsqrt(jnp.mean(t * t, axis=-1, keepdims=True) + NORM_EPS)


def _const_spec(shape):
    return pl.BlockSpec(shape, lambda *_: (0,) * len(shape), pipeline_mode=pl.Buffered(1))


def _mod_kernel(c_ref, w_ref, b_ref, o_ref):
    c = c_ref[...]
    a = c * _sigmoid(c)
    ah, al = _split2(a)
    wh, wl = _split2(w_ref[...])
    o_ref[...] = _dot(ah, wh) + _dot(ah, wl) + _dot(al, wh) + b_ref[...]


def _mod(c_pad, ada_w, ada_b):
    rows, d = c_pad.shape
    n = ada_w.shape[1]
    tn = 768
    return pl.pallas_call(
        _mod_kernel,
        out_shape=jax.ShapeDtypeStruct((rows, n), F32),
        grid=(n // tn,),
        in_specs=[
            pl.BlockSpec((rows, d), lambda j: (0, 0)),
            pl.BlockSpec((d, tn), lambda j: (0, j)),
            pl.BlockSpec((1, tn), lambda j: (0, j)),
        ],
        out_specs=pl.BlockSpec((rows, tn), lambda j: (0, j)),
        name="mod",
    )(c_pad, ada_w, ada_b)


def _inproj_kernel(x_ref, pos_ref, sc_ref, sh_ref, g_ref, inv_ref, wa_ref, wr_ref, wg_ref, mu_ref,
                   a0_ref, a1_ref, a2_ref, rr_ref, rk_ref, rv_ref, rl_ref, gate_ref, carry_ref,
                   *, tiles_per_seq):
    i = pl.program_id(0)
    tm = x_ref.shape[0]
    d = x_ref.shape[1]
    h = _rms(x_ref[...]) * g_ref[...]
    hb = (h * (1.0 + sc_ref[...]) + sh_ref[...]).astype(BF16)

    ang = pos_ref[...] * inv_ref[...]
    cos = jnp.cos(ang)
    sin = jnp.sin(ang)
    lane = lax.broadcasted_iota(jnp.int32, (1, LANES), 1) % HEAD_DIM
    half = ROPE_DIM // 2
    cc = jnp.where(lane < ROPE_DIM, cos, 1.0)
    s_lo = jnp.where(lane < half, -sin, 0.0)
    s_hi = jnp.where((lane >= half) & (lane < ROPE_DIM), sin, 0.0)

    def rope(t):
        return t * cc + pltpu.roll(t, LANES - half, 1) * s_lo + pltpu.roll(t, half, 1) * s_hi

    a_refs = (a0_ref, a1_ref, a2_ref)
    aw = len(ATTN_GROUPS) * GROUP_WIDTH
    for which in range(3):
        t = _dot(hb, wa_ref[:, which * aw:(which + 1) * aw])
        for g in range(len(ATTN_GROUPS)):
            tg = t[:, g * GROUP_WIDTH:(g + 1) * GROUP_WIDTH]
            if which < 2:
                tg = jnp.concatenate([rope(tg[:, :LANES]), rope(tg[:, LANES:])], axis=1)
            if which == 0:
                tg = tg * (HEAD_DIM ** -0.5)
            a_refs[g][:, which * GROUP_WIDTH:(which + 1) * GROUP_WIDTH] = tg.astype(BF16)

    first = (i % tiles_per_seq) == 0
    row0 = lax.broadcasted_iota(jnp.int32, (tm, 1), 0) == 0
    col = 0
    for ref in (rr_ref, rk_ref, rv_ref, rl_ref):
        n = ref.shape[1]
        u = _dot(hb, wr_ref[:, col:col + n])
        carry = jnp.where(first, 0.0, carry_ref[:, col:col + n])
        prev = jnp.where(row0, carry, pltpu.roll(u, 1, 0))
        carry_ref[:, col:col + n] = u[tm - 1:tm, :]
        ref[...] = (u + (prev - u) * mu_ref[:, col:col + n]).astype(ref.dtype)
        col += n

    for j in range(2):
        gj = _dot(hb, wg_ref[:, j * d:(j + 1) * d])
        gate_ref[:, j * d:(j + 1) * d] = _sigmoid(gj).astype(BF16)


def _inproj(x2, posf, sc, sh, gain, inv_lane, wa, wr, wg, mu, *, seq, tm):
    t_rows, d = x2.shape
    tiles_per_seq = seq // tm
    aw = wa.shape[1] // 3
    shift_w = wr.shape[1]
    lora_w = shift_w - 3 * d
    row = lambda i: (i, 0)
    per_batch = lambda i: (i // tiles_per_seq, 0, 0)
    out_shape = (
        [jax.ShapeDtypeStruct((t_rows, aw), BF16)] * 3
        + [jax.ShapeDtypeStruct((t_rows, d), BF16)] * 3
        + [jax.ShapeDtypeStruct((t_rows, lora_w), F32), jax.ShapeDtypeStruct((t_rows, 2 * d), BF16)]
    )
    out_specs = (
        [pl.BlockSpec((tm, aw), row)] * 3
        + [pl.BlockSpec((tm, d), row)] * 3
        + [pl.BlockSpec((tm, lora_w), row), pl.BlockSpec((tm, 2 * d), row)]
    )
    return pl.pallas_call(
        functools.partial(_inproj_kernel, tiles_per_seq=tiles_per_seq),
        out_shape=out_shape,
        grid=(t_rows // tm,),
        in_specs=[
            pl.BlockSpec((tm, d), row),
            pl.BlockSpec((tm, 1), row),
            pl.BlockSpec((None, 1, d), per_batch),
            pl.BlockSpec((None, 1, d), per_batch),
            _const_spec((1, d)),
            _const_spec((1, LANES)),
            _const_spec(wa.shape),
            _const_spec(wr.shape),
            _const_spec(wg.shape),
            _const_spec((1, shift_w)),
        ],
        out_specs=out_specs,
        scratch_shapes=[pltpu.VMEM((1, shift_w), F32)],
        compiler_params=pltpu.CompilerParams(
            dimension_semantics=("arbitrary",), vmem_limit_bytes=VMEM_LIMIT),
        name="inproj",
    )(x2, posf, sc, sh, gain, inv_lane, wa, wr, wg, mu)


def _attn_kernel(*refs, blocks_per_subseq):
    n_groups = len(blocks_per_subseq)
    in_refs, out_refs = refs[:2 * n_groups], refs[2 * n_groups:]
    m = pl.program_id(1)
    blk = ATTN_BLOCK
    gw = GROUP_WIDTH
    row = lax.broadcasted_iota(jnp.int32, (blk, blk), 0)
    colk = lax.broadcasted_iota(jnp.int32, (blk, blk), 1)
    mask_cur = colk <= row
    mask_prev_band = colk >= row
    head_of_lane = lax.broadcasted_iota(jnp.int32, (1, gw), 1) // HEAD_DIM
    for g in range(n_groups):
        cur_ref, prev_ref = in_refs[2 * g], in_refs[2 * g + 1]
        o_ref, l_ref = out_refs[2 * g], out_refs[2 * g + 1]
        has_prev = (m % blocks_per_subseq[g]) > 0
        mask_prev = mask_prev_band & has_prev
        q = cur_ref[:, 0:gw]
        k = cur_ref[:, gw:2 * gw]
        v = cur_ref[:, 2 * gw:3 * gw]
        kp = prev_ref[:, gw:2 * gw]
        vp = prev_ref[:, 2 * gw:3 * gw]
        o_acc = jnp.zeros((blk, gw), F32)
        l_acc = jnp.zeros((blk, gw), F32)
        for hd in range(ATTN_HEADS_PER_GROUP):
            hm = head_of_lane == hd
            qh = jnp.where(hm, q, jnp.zeros_like(q))
            sc = jnp.where(mask_cur, _dot_nt(qh, k), NEG_INF)
            sp = jnp.where(mask_prev, _dot_nt(qh, kp), NEG_INF)
            mx = jnp.maximum(jnp.max(sc, axis=-1, keepdims=True), jnp.max(sp, axis=-1, keepdims=True))
            pc = jnp.exp(sc - mx)
            pp = jnp.exp(sp - mx)
            den = jnp.sum(pc, axis=-1, keepdims=True) + jnp.sum(pp, axis=-1, keepdims=True)
            pv = _dot(pc.astype(BF16), v) + _dot(pp.astype(BF16), vp)
            o_acc = jnp.where(hm, pv / den, o_acc)
            l_acc = jnp.where(hm, mx + jnp.log(den), l_acc)
        o_ref[...] = o_acc.astype(BF16)
        l_ref[...] = l_acc


def _attn(a_list, *, batch, seq):
    assert all(w // dil == ATTN_BLOCK for w, dil in ATTN_GROUPS)
    blk, gw = ATTN_BLOCK, GROUP_WIDTH
    n_blocks = seq // blk
    ins, in_specs, out_shape, out_specs, nbs = [], [], [], [], []
    for a, (_, dil) in zip(a_list, ATTN_GROUPS):
        assert seq % (dil * blk) == 0
        nb = n_blocks // dil
        nbs.append(nb)
        view = a.reshape(batch, seq // dil, dil * 3 * gw)
        cur = lambda b, m, nb=nb: (b, m % nb, m // nb)
        prev = lambda b, m, nb=nb: (b, jnp.maximum(m % nb - 1, 0), m // nb)
        ins += [view, view]
        in_specs += [pl.BlockSpec((None, blk, 3 * gw), cur), pl.BlockSpec((None, blk, 3 * gw), prev)]
        out_shape += [jax.ShapeDtypeStruct((batch, seq // dil, dil * gw), BF16),
                      jax.ShapeDtypeStruct((batch, seq // dil, dil * gw), F32)]
        out_specs += [pl.BlockSpec((None, blk, gw), cur)] * 2
    outs = pl.pallas_call(
        functools.partial(_attn_kernel, blocks_per_subseq=tuple(nbs)),
        out_shape=out_shape,
        grid=(batch, n_blocks),
        in_specs=in_specs,
        out_specs=out_specs,
        compiler_params=pltpu.CompilerParams(dimension_semantics=("arbitrary", "arbitrary")),
        name="attn",
    )(*ins)
    return [o.reshape(batch * seq, gw) for o in outs]


def _seg_sum(z, seg_ones):
    rows, c = z.shape
    n = c // LANES
    zs = jnp.concatenate([z[:, j * LANES:(j + 1) * LANES] for j in range(n)], axis=0).astype(BF16)
    out = _dot(zs, seg_ones)
    return jnp.concatenate([out[j * rows:(j + 1) * rows] for j in range(n)], axis=1)


def _wkv_kernel(rr_ref, rk_ref, rv_ref, rl_ref, w0_ref, w2_ref, a0_ref, a2_ref, g2_ref, kk_ref, ka_ref,
                rkp_ref, gnw_ref, gnb_ref, o_ref,
                state_ref, at_ref, rt_ref, bt_ref, kt_ref, v_ref, gam_ref, bonus_ref, g_ref):
    tt, c = rr_ref.shape
    n_pairs = c // LANES
    ck = WKV_CHUNK
    n_chunks = tt // ck
    staged = (at_ref, rt_ref, bt_ref, kt_ref, v_ref, gam_ref, bonus_ref, g_ref)

    @pl.when(pl.program_id(1) == 0)
    def _():
        state_ref[...] = jnp.zeros_like(state_ref)
        for ref in staged:
            ref[...] = jnp.zeros_like(ref)

    at_n, rt_n, bt_n, kt_n, v_n, gam_n, bonus_p, g_p = [ref[...] for ref in staged]

    lane = lax.broadcasted_iota(jnp.int32, (LANES, LANES), 1)
    rowi = lax.broadcasted_iota(jnp.int32, (LANES, LANES), 0)
    seg_ones = jnp.where(lane // HEAD_DIM == rowi // HEAD_DIM, 1.0, 0.0).astype(BF16)

    def stage_tile():
        r = rr_ref[...].astype(F32)
        k = rk_ref[...].astype(F32)
        v = rv_ref[...]
        lora = rl_ref[...]
        dwa = lora[:, 0:LANES]
        dg = lora[:, LANES:2 * LANES]
        zw = w0_ref[...] + _dot(jnp.tanh(dwa).astype(BF16), w2_ref[...])
        w_log = -(jnp.maximum(-zw, 0.0) + jnp.log(1.0 + jnp.exp(-jnp.abs(zw)))) - 0.5
        lw = -jnp.exp(w_log)
        a = _sigmoid(a0_ref[...] + _dot(dwa.astype(BF16), a2_ref[...]))
        g = _dot(_sigmoid(dg).astype(BF16), g2_ref[...])
        kk = k * kk_ref[...]
        kk = kk * lax.rsqrt(jnp.maximum(_seg_sum(kk * kk, seg_ones), 1e-24))
        kmod = k * (1.0 + (a - 1.0) * ka_ref[...])
        bonus = _seg_sum(r * kmod * rkp_ref[...], seg_ones) * v.astype(F32)
        tri = (lax.broadcasted_iota(jnp.int32, (ck, ck), 1)
               <= lax.broadcasted_iota(jnp.int32, (ck, ck), 0)).astype(BF16)
        hi, lo = _split2(lw)
        cum = jnp.concatenate([_dot(tri, hi[ci * ck:(ci + 1) * ck]) + _dot(tri, lo[ci * ck:(ci + 1) * ck])
                               for ci in range(n_chunks)], axis=0)
        e_in = jnp.exp(cum)
        e_neg = jnp.exp(-cum)
        return (
            (-kk * jnp.exp(cum - lw)).astype(BF16),
            (r * e_in).astype(BF16),
            (kk * a * e_neg).astype(BF16),
            (kmod * e_neg).astype(BF16),
            v,
            jnp.concatenate([e_in[(ci + 1) * ck - 8:(ci + 1) * ck] for ci in range(n_chunks)], axis=0),
            bonus,
            g,
        )

    t_idx = lax.broadcasted_iota(jnp.int32, (ck, LANES), 0)
    i_idx = lax.broadcasted_iota(jnp.int32, (ck, LANES), 1) % ck
    strict = i_idx < t_idx
    incl = i_idx <= t_idx
    lo_mask = lax.broadcasted_iota(jnp.int32, (1, LANES), 1) < HEAD_DIM
    eye_cat = jnp.where(i_idx == t_idx, 1.0, 0.0)
    bk_row_head = lax.broadcasted_iota(jnp.int32, (LANES, 2 * LANES), 0) // HEAD_DIM
    bk_col_head = (lax.broadcasted_iota(jnp.int32, (LANES, 2 * LANES), 1) // ck) % 2
    bk_mask = bk_row_head == bk_col_head

    def entries(t):
        return jnp.stack([t[ci * ck:(ci + 1) * ck, p * LANES:(p + 1) * LANES]
                          for ci in range(n_chunks) for p in range(n_pairs)])

    def stack_heads(t):
        z = jnp.zeros_like(t)
        return jnp.concatenate([jnp.where(lo_mask, t, z), jnp.where(lo_mask, z, t)], axis=1).astype(BF16)

    def bmm(a_, b_):
        return jnp.einsum('pmk,pkn->pmn', a_, b_, preferred_element_type=F32)

    at = entries(at_n)
    rt = entries(rt_n)
    bt = entries(bt_n)
    kt = entries(kt_n)
    bk_t = jnp.swapaxes(jnp.concatenate([bt, bt, kt, kt], axis=1), 1, 2)
    bk_t = jnp.where(bk_mask, bk_t, jnp.zeros_like(bk_t))
    vw = stack_heads(entries(v_n))
    gcol = jnp.swapaxes(
        jnp.stack([gam_n[ci * 8:(ci + 1) * 8, p * LANES:(p + 1) * LANES]
                   for ci in range(n_chunks) for p in range(n_pairs)]), 1, 2)[:, :, 7:8]

    gm = bmm(jnp.concatenate([at, rt], axis=1), bk_t)
    a_ab = jnp.where(strict, gm[:, :ck, :LANES], 0.0)
    a_ak = jnp.where(strict, gm[:, :ck, LANES:], 0.0).astype(BF16)
    a_rb = jnp.where(incl, gm[:, ck:, :LANES], 0.0).astype(BF16)
    a_rk = jnp.where(incl, gm[:, ck:, LANES:], 0.0).astype(BF16)
    new_staged = stage_tile()
    tm_ = eye_cat + a_ab
    pw = bmm(a_ab.astype(BF16), stack_heads(a_ab))
    for _ in range(4):
        both = bmm(jnp.concatenate([tm_, pw], axis=1).astype(BF16), stack_heads(pw))
        tm_ = tm_ + both[:, :ck, :]
        pw = both[:, ck:, :]
    tm_ = tm_ + bmm(tm_.astype(BF16), stack_heads(pw))
    t_cat = tm_.astype(BF16)
    by_v = bmm(jnp.concatenate([a_ak, a_rk, bk_t[:, :, LANES:]], axis=1), vw)
    ta = bmm(t_cat, stack_heads(at))
    tx = bmm(t_cat, stack_heads(by_v[:, :ck, :]))
    ra = rt + bmm(a_rb, stack_heads(ta))
    yc = by_v[:, ck:2 * ck, :] + bmm(a_rb, stack_heads(tx))
    dk = by_v[:, 2 * ck:, :]
    tara = jnp.concatenate([ta, ra], axis=1).astype(BF16)
    b_t = bk_t[:, :, :LANES]

    s = state_ref[...]
    y_rows = []
    for ci in range(n_chunks):
        es = slice(ci * n_pairs, (ci + 1) * n_pairs)
        o1 = bmm(tara[es], s.astype(BF16))
        u = o1[:, :ck, :] + tx[es]
        y = o1[:, ck:, :] + yc[es]
        y_rows.append(jnp.concatenate([y[p] for p in range(n_pairs)], axis=1))
        s = (s + bmm(b_t[es], stack_heads(u)) + dk[es]) * gcol[es]
    state_ref[...] = s

    y = jnp.concatenate(y_rows, axis=0)
    inv_n = 1.0 / HEAD_DIM
    mean = _seg_sum(y, seg_ones) * inv_n
    ycen = y - mean
    var = _seg_sum(ycen * ycen, seg_ones) * inv_n
    yn = ycen * lax.rsqrt(var + GN_EPS) * gnw_ref[...] + gnb_ref[...]
    o_ref[...] = ((yn + bonus_p) * g_p).astype(o_ref.dtype)

    for ref, val in zip(staged, new_staged):
        ref[...] = val


def _wkv(rr, rk, rv, rl, w0, w2p, a0, a2p, g2, k_k, k_a, r_k, gn_w, gn_b, *, batch, seq, tt):
    t_rows, c = rr.shape
    steps = seq // tt
    row_in = lambda b, j: (b * steps + jnp.minimum(j, steps - 1), 0)
    row_out = lambda b, j: (b * steps + jnp.maximum(j - 1, 0), 0)
    vec = _const_spec((1, c))
    n_gam = (tt // WKV_CHUNK) * 8
    return pl.pallas_call(
        _wkv_kernel,
        out_shape=jax.ShapeDtypeStruct((t_rows, c), BF16),
        grid=(batch, steps + 1),
        in_specs=[
            pl.BlockSpec((tt, c), row_in), pl.BlockSpec((tt, c), row_in), pl.BlockSpec((tt, c), row_in),
            pl.BlockSpec((tt, rl.shape[1]), row_in),
            vec, _const_spec(w2p.shape), vec, _const_spec(a2p.shape), _const_spec(g2.shape),
            vec, vec, vec, vec, vec,
        ],
        out_specs=pl.BlockSpec((tt, c), row_out),
        scratch_shapes=[pltpu.VMEM((c // LANES, LANES, LANES), F32)]
        + [pltpu.VMEM((tt, c), BF16)] * 5
        + [pltpu.VMEM((n_gam, c), F32), pltpu.VMEM((tt, c), F32), pltpu.VMEM((tt, c), F32)],
        compiler_params=pltpu.CompilerParams(
            dimension_semantics=("arbitrary", "arbitrary"), vmem_limit_bytes=VMEM_LIMIT),
        name="wkv",
    )(rr, rk, rv, rl, w0, w2p, a0, a2p, g2, k_k, k_a, r_k, gn_w, gn_b)


def _branch_kernel(o0_ref, l0_ref, o1_ref, l1_ref, o2_ref, l2_ref, gate_ref, orw_ref, x_ref,
                   wba_ref, wbr_ref, wo_ref, gt_ref, sc_ref, sh_ref, gpost_ref, gpre_ref,
                   x1_ref, h2_ref):
    d = x_ref.shape[1]
    l0, l1, l2 = l0_ref[...], l1_ref[...], l2_ref[...]
    mx = jnp.maximum(jnp.maximum(l0, l1), l2)
    e0, e1, e2 = jnp.exp(l0 - mx), jnp.exp(l1 - mx), jnp.exp(l2 - mx)
    o_attn = (e0 * o0_ref[...].astype(F32) + e1 * o1_ref[...].astype(F32)
              + e2 * o2_ref[...].astype(F32)) / (e0 + e1 + e2)
    ya = _dot(o_attn.astype(BF16), wba_ref[...])
    yr = _dot(orw_ref[...], wbr_ref[...])
    merged = gate_ref[:, 0:d].astype(F32) * ya + gate_ref[:, d:2 * d].astype(F32) * yr
    mix = _dot(merged.astype(BF16), wo_ref[...])
    x1 = x_ref[...] + gt_ref[...] * (_rms(mix) * gpost_ref[...])
    x1_ref[...] = x1
    h2 = _rms(x1) * gpre_ref[...]
    h2_ref[...] = (h2 * (1.0 + sc_ref[...]) + sh_ref[...]).astype(BF16)


def _branch(attn_outs, gates, o_rwkv, x2, wba, wbr, wo, gt, sc, sh, gpost, gpre, *, seq, tm):
    t_rows, d = x2.shape
    tiles_per_seq = seq // tm
    row = lambda i: (i, 0)
    per_batch = lambda i: (i // tiles_per_seq, 0, 0)
    gw = GROUP_WIDTH
    return pl.pallas_call(
        _branch_kernel,
        out_shape=[jax.ShapeDtypeStruct((t_rows, d), F32), jax.ShapeDtypeStruct((t_rows, d), BF16)],
        grid=(t_rows // tm,),
        in_specs=[pl.BlockSpec((tm, gw), row)] * 6 + [
            pl.BlockSpec((tm, 2 * d), row), pl.BlockSpec((tm, d), row), pl.BlockSpec((tm, d), row),
            _const_spec(wba.shape), _const_spec(wbr.shape), _const_spec(wo.shape),
            pl.BlockSpec((None, 1, d), per_batch), pl.BlockSpec((None, 1, d), per_batch),
            pl.BlockSpec((None, 1, d), per_batch),
            _const_spec((1, d)), _const_spec((1, d)),
        ],
        out_specs=[pl.BlockSpec((tm, d), row), pl.BlockSpec((tm, d), row)],
        compiler_params=pltpu.CompilerParams(
            dimension_semantics=("arbitrary",), vmem_limit_bytes=VMEM_LIMIT),
        name="branch",
    )(*attn_outs, gates, o_rwkv, x2, wba, wbr, wo, gt, sc, sh, gpost, gpre)


def _ffn_kernel(h_ref, x_ref, w1_ref, w2_ref, gt_ref, gpost_ref, o_ref, *, tf):
    h = h_ref[...]
    acc = jnp.zeros(x_ref.shape, F32)
    for j in range(w1_ref.shape[1] // tf):
        a = jnp.maximum(_dot(h, w1_ref[:, j * tf:(j + 1) * tf]), 0.0)
        acc = acc + _dot((a * a).astype(BF16), w2_ref[j * tf:(j + 1) * tf, :])
    o_ref[...] = x_ref[...] + gt_ref[...] * (_rms(acc) * gpost_ref[...])


def _ffn(h2, x1, w1, w2, gt, gpost, *, seq, tm, tf):
    t_rows, d = x1.shape
    tiles_per_seq = seq // tm
    row = lambda i: (i, 0)
    per_batch = lambda i: (i // tiles_per_seq, 0, 0)
    return pl.pallas_call(
        functools.partial(_ffn_kernel, tf=tf),
        out_shape=jax.ShapeDtypeStruct((t_rows, d), F32),
        grid=(t_rows // tm,),
        in_specs=[
            pl.BlockSpec((tm, d), row), pl.BlockSpec((tm, d), row),
            _const_spec(w1.shape), _const_spec(w2.shape),
            pl.BlockSpec((None, 1, d), per_batch), _const_spec((1, d)),
        ],
        out_specs=pl.BlockSpec((tm, d), row),
        compiler_params=pltpu.CompilerParams(
            dimension_semantics=("arbitrary",), vmem_limit_bytes=VMEM_LIMIT),
        name="ffn",
    )(h2, x1, w1, w2, gt, gpost)


def kernel(x, c, positions, ada_w, ada_b, norm_mix_pre, norm_mix_post, norm_ffn_pre, norm_ffn_post, w_in, shift_mu, decay_w0, decay_w2, iclr_a0, iclr_a2, gate_g2, k_k, k_a, r_k, gn_w, gn_b, w_branch, w_out, w_ff1, w_ff2):
    batch, seq, d = x.shape
    depth = ada_w.shape[0]
    t_rows = batch * seq
    n_groups = len(ATTN_GROUPS)
    attn_w = n_groups * GROUP_WIDTH
    attn_out_w = GROUP_WIDTH
    shift_w = shift_mu.shape[1]
    assert shift_w == 3 * d + DECAY_LORA + ICLR_LORA + GATE_LORA
    assert DECAY_LORA + ICLR_LORA == LANES and GATE_LORA == LANES
    tm = 512

    x2 = x.reshape(t_rows, d)
    posf = positions.astype(F32).reshape(t_rows, 1)
    half = ROPE_DIM // 2
    inv = ROPE_THETA ** (-jnp.arange(half, dtype=F32) * 2.0 / ROPE_DIM)
    inv_lane = jnp.tile(inv, LANES // half).reshape(1, LANES)
    c_pad = jnp.pad(c, ((0, 8 - batch), (0, 0)))
    vec = lambda p: p.reshape(1, -1)

    for l in range(depth):
        mod = _mod(c_pad, ada_w[l], ada_b[l].reshape(1, -1))[:batch]
        sh1, sc1, gt1, sh2, sc2, gt2 = [m.reshape(batch, 1, d) for m in jnp.split(mod, 6, axis=-1)]

        w = w_in[l].astype(BF16)
        wa = w[:, :3 * attn_w]
        wr = w[:, 3 * attn_w:3 * attn_w + shift_w]
        wg = w[:, 3 * attn_w + shift_w:]
        a0, a1, a2, rr, rk, rv, rl, gates = _inproj(
            x2, posf, sc1, sh1, vec(norm_mix_pre[l]), inv_lane, wa, wr, wg, vec(shift_mu[l]),
            seq=seq, tm=tm)

        attn_outs = _attn([a0, a1, a2], batch=batch, seq=seq)

        zeros = jnp.zeros((DECAY_LORA, d), BF16)
        w2p = jnp.concatenate([decay_w2[l].astype(BF16), zeros], axis=0)
        a2p = jnp.concatenate([zeros, iclr_a2[l].astype(BF16)], axis=0)
        o_rwkv = _wkv(rr, rk, rv, rl, vec(decay_w0[l]), w2p, vec(iclr_a0[l]), a2p,
                      gate_g2[l].astype(BF16), vec(k_k[l]), vec(k_a[l]), vec(r_k[l]),
                      vec(gn_w[l]), vec(gn_b[l]), batch=batch, seq=seq, tt=256)

        wb = w_branch[l].astype(BF16)
        x1, h2 = _branch(attn_outs, gates, o_rwkv, x2, wb[:attn_out_w], wb[attn_out_w:],
                         w_out[l].astype(BF16), gt1, sc2, sh2, vec(norm_mix_post[l]),
                         vec(norm_ffn_pre[l]), seq=seq, tm=tm)
        x2 = _ffn(h2, x1, w_ff1[l].astype(BF16), w_ff2[l].astype(BF16), gt2, vec(norm_ffn_post[l]),
                  seq=seq, tm=tm, tf=512)
    return x2.reshape(batch, seq, d)
```

```python
import functools

import jax
import jax.numpy as jnp
from jax import lax
from jax.experimental import pallas as pl
from jax.experimental.pallas import tpu as pltpu

F32 = jnp.float32
BF16 = jnp.bfloat16

HEAD_DIM = 64
ATTN_GROUPS = ((128, 1), (512, 4), (2048, 16))
ATTN_HEADS_PER_GROUP = 4
ATTN_BLOCK = 128
GROUP_WIDTH = ATTN_HEADS_PER_GROUP * HEAD_DIM
ROPE_THETA = 500000.0
ROPE_DIM = HEAD_DIM // 4
DECAY_LORA = 64
ICLR_LORA = 64
GATE_LORA = 128
NORM_EPS = 1e-6
GN_EPS = 64e-5
NEG_INF = -1e30

LANES = 128
WKV_CHUNK = 64
VMEM_LIMIT = 56 * 1024 * 1024


def _dot(a, b):
    return jnp.dot(a, b, preferred_element_type=F32)


def _dot_nt(a, b):
    return lax.dot_general(a, b, (((1,), (1,)), ((), ())), preferred_element_type=F32)


def _split2(x):
    hi = x.astype(BF16)
    lo = (x - hi.astype(F32)).astype(BF16)
    return hi, lo


def _sigmoid(x):
    return 1.0 / (1.0 + jnp.exp(-x))


def _rms(t):
    return t * lax.rsqrt(jnp.mean(t * t, axis=-1, keepdims=True) + NORM_EPS)


def _const_spec(shape):
    return pl.BlockSpec(shape, lambda *_: (0,) * len(shape), pipeline_mode=pl.Buffered(1))


def _mod_kernel(c_ref, w_ref, b_ref, o_ref):
    c = c_ref[...]
    a = c * _sigmoid(c)
    ah, al = _split2(a)
    wh, wl = _split2(w_ref[...])
    o_ref[...] = _dot(ah, wh) + _dot(ah, wl) + _dot(al, wh) + b_ref[...]


def _mod(c_pad, ada_w, ada_b):
    rows, d = c_pad.shape
    n = ada_w.shape[1]
    tn = 768
    return pl.pallas_call(
        _mod_kernel,
        out_shape=jax.ShapeDtypeStruct((rows, n), F32),
        grid=(n // tn,),
        in_specs=[
            pl.BlockSpec((rows, d), lambda j: (0, 0)),
            pl.BlockSpec((d, tn), lambda j: (0, j)),
            pl.BlockSpec((1, tn), lambda j: (0, j)),
        ],
        out_specs=pl.BlockSpec((rows, tn), lambda j: (0, j)),
        name="mod",
    )(c_pad, ada_w, ada_b)


def _inproj_kernel(x_ref, pos_ref, sc_ref, sh_ref, g_ref, inv_ref, wa_ref, wr_ref, wg_ref, mu_ref,
                   a0_ref, a1_ref, a2_ref, rr_ref, rk_ref, rv_ref, rl_ref, gate_ref, carry_ref, dil_ref,
                   *, tiles_per_seq):
    i = pl.program_id(0)
    tm = x_ref.shape[0]
    d = x_ref.shape[1]
    h = _rms(x_ref[...]) * g_ref[...]
    hb = (h * (1.0 + sc_ref[...]) + sh_ref[...]).astype(BF16)

    ang = pos_ref[...] * inv_ref[...]
    cos = jnp.cos(ang)
    sin = jnp.sin(ang)
    lane = lax.broadcasted_iota(jnp.int32, (1, LANES), 1) % HEAD_DIM
    half = ROPE_DIM // 2
    cc = jnp.where(lane < ROPE_DIM, cos, 1.0)
    s_lo = jnp.where(lane < half, -sin, 0.0)
    s_hi = jnp.where((lane >= half) & (lane < ROPE_DIM), sin, 0.0)

    def rope(t):
        return t * cc + pltpu.roll(t, LANES - half, 1) * s_lo + pltpu.roll(t, half, 1) * s_hi

    a_refs = (a0_ref, a1_ref, a2_ref)
    aw = len(ATTN_GROUPS) * GROUP_WIDTH
    gw = GROUP_WIDTH
    slab = 0
    for which in range(3):
        t = _dot(hb, wa_ref[:, which * aw:(which + 1) * aw])
        for g, (_, dil) in enumerate(ATTN_GROUPS):
            halves = [t[:, g * gw + s * LANES:g * gw + (s + 1) * LANES] for s in range(gw // LANES)]
            if which < 2:
                halves = [rope(hv) for hv in halves]
            if which == 0:
                halves = [hv * (HEAD_DIM ** -0.5) for hv in halves]
            for s, hv in enumerate(halves):
                if dil == 1:
                    a_refs[g][:, which * gw + s * LANES:which * gw + (s + 1) * LANES] = hv.astype(BF16)
                    continue
                dil_ref[slab] = hv
                for r in range(dil):
                    col = r * 3 * gw + which * gw + s * LANES
                    a_refs[g][:, col:col + LANES] = dil_ref[slab, pl.ds(r, tm // dil, stride=dil), :].astype(BF16)
                slab += 1

    first = (i % tiles_per_seq) == 0
    row0 = lax.broadcasted_iota(jnp.int32, (tm, 1), 0) == 0
    col = 0
    for ref in (rr_ref, rk_ref, rv_ref, rl_ref):
        n = ref.shape[1]
        u = _dot(hb, wr_ref[:, col:col + n])
        carry = jnp.where(first, 0.0, carry_ref[:, col:col + n])
        prev = jnp.where(row0, carry, pltpu.roll(u, 1, 0))
        carry_ref[:, col:col + n] = u[tm - 1:tm, :]
        ref[...] = (u + (prev - u) * mu_ref[:, col:col + n]).astype(ref.dtype)
        col += n

    for j in range(2):
        gj = _dot(hb, wg_ref[:, j * d:(j + 1) * d])
        gate_ref[:, j * d:(j + 1) * d] = _sigmoid(gj).astype(BF16)


def _inproj(x2, posf, sc, sh, gain, inv_lane, wa, wr, wg, mu, *, seq, tm):
    t_rows, d = x2.shape
    tiles_per_seq = seq // tm
    aw = wa.shape[1] // 3
    shift_w = wr.shape[1]
    lora_w = shift_w - 3 * d
    row = lambda i: (i, 0)
    per_batch = lambda i: (i // tiles_per_seq, 0, 0)
    dils = [dil for _, dil in ATTN_GROUPS]
    out_shape = (
        [jax.ShapeDtypeStruct((t_rows // dil, dil * aw), BF16) for dil in dils]
        + [jax.ShapeDtypeStruct((t_rows, d), BF16)] * 3
        + [jax.ShapeDtypeStruct((t_rows, lora_w), F32), jax.ShapeDtypeStruct((t_rows, 2 * d), BF16)]
    )
    out_specs = (
        [pl.BlockSpec((tm // dil, dil * aw), row) for dil in dils]
        + [pl.BlockSpec((tm, d), row)] * 3
        + [pl.BlockSpec((tm, lora_w), row), pl.BlockSpec((tm, 2 * d), row)]
    )
    n_slabs = sum(3 * GROUP_WIDTH // LANES for dil in dils if dil > 1)
    return pl.pallas_call(
        functools.partial(_inproj_kernel, tiles_per_seq=tiles_per_seq),
        out_shape=out_shape,
        grid=(t_rows // tm,),
        in_specs=[
            pl.BlockSpec((tm, d), row),
            pl.BlockSpec((tm, 1), row),
            pl.BlockSpec((None, 1, d), per_batch),
            pl.BlockSpec((None, 1, d), per_batch),
            _const_spec((1, d)),
            _const_spec((1, LANES)),
            _const_spec(wa.shape),
            _const_spec(wr.shape),
            _const_spec(wg.shape),
            _const_spec((1, shift_w)),
        ],
        out_specs=out_specs,
        scratch_shapes=[pltpu.VMEM((1, shift_w), F32), pltpu.VMEM((n_slabs, tm, LANES), F32)],
        compiler_params=pltpu.CompilerParams(
            dimension_semantics=("arbitrary",), vmem_limit_bytes=VMEM_LIMIT),
        name="inproj",
    )(x2, posf, sc, sh, gain, inv_lane, wa, wr, wg, mu)


def _attn_kernel(*refs, blocks_per_subseq):
    n_groups = len(blocks_per_subseq)
    in_refs, out_refs = refs[:2 * n_groups], refs[2 * n_groups:]
    m = pl.program_id(1)
    blk = ATTN_BLOCK
    gw = GROUP_WIDTH
    nh = ATTN_HEADS_PER_GROUP
    row = lax.broadcasted_iota(jnp.int32, (nh * blk, blk), 0) % blk
    colk = lax.broadcasted_iota(jnp.int32, (nh * blk, blk), 1)
    mask_cur = colk <= row
    mask_prev_band = colk >= row
    head_of_lane = lax.broadcasted_iota(jnp.int32, (1, gw), 1) // HEAD_DIM
    groups = range(n_groups)
    cur = [in_refs[2 * g] for g in groups]
    prev = [in_refs[2 * g + 1] for g in groups]
    q4, sc, sp = [], [], []
    for g in groups:
        q = cur[g][:, 0:gw]
        zq = jnp.zeros_like(q)
        q4.append(jnp.concatenate([jnp.where(head_of_lane == hd, q, zq) for hd in range(nh)], axis=0))
    for g in groups:
        mask_prev = mask_prev_band & ((m % blocks_per_subseq[g]) > 0)
        sc.append(jnp.where(mask_cur, _dot_nt(q4[g], cur[g][:, gw:2 * gw]), NEG_INF))
        sp.append(jnp.where(mask_prev, _dot_nt(q4[g], prev[g][:, gw:2 * gw]), NEG_INF))
    mx = [jnp.max(jnp.maximum(sc[g], sp[g]), axis=-1, keepdims=True) for g in groups]
    pc = [jnp.exp(sc[g] - mx[g]) for g in groups]
    pp = [jnp.exp(sp[g] - mx[g]) for g in groups]
    den = [jnp.sum(pc[g] + pp[g], axis=-1, keepdims=True) for g in groups]
    pv = [_dot(pc[g].astype(BF16), cur[g][:, 2 * gw:3 * gw]) + _dot(pp[g].astype(BF16), prev[g][:, 2 * gw:3 * gw])
          for g in groups]
    for g in groups:
        o4 = pv[g] * (1.0 / den[g])
        l4 = mx[g] + jnp.log(den[g])
        o_acc = o4[0:blk]
        l_acc = jnp.broadcast_to(l4[0:blk], (blk, gw))
        for hd in range(1, nh):
            hm = head_of_lane == hd
            o_acc = jnp.where(hm, o4[hd * blk:(hd + 1) * blk], o_acc)
            l_acc = jnp.where(hm, l4[hd * blk:(hd + 1) * blk], l_acc)
        out_refs[2 * g][...] = o_acc.astype(BF16)
        out_refs[2 * g + 1][...] = l_acc


def _attn(a_list, *, batch, seq):
    assert all(w // dil == ATTN_BLOCK for w, dil in ATTN_GROUPS)
    blk, gw = ATTN_BLOCK, GROUP_WIDTH
    n_blocks = seq // blk
    ins, in_specs, out_shape, out_specs, nbs = [], [], [], [], []
    for a, (_, dil) in zip(a_list, ATTN_GROUPS):
        assert seq % (dil * blk) == 0
        nb = n_blocks // dil
        nbs.append(nb)
        view = a.reshape(batch, seq // dil, dil * 3 * gw)
        cur = lambda b, m, nb=nb: (b, m % nb, m // nb)
        prev = lambda b, m, nb=nb: (b, jnp.maximum(m % nb - 1, 0), m // nb)
        ins += [view, view]
        in_specs += [pl.BlockSpec((None, blk, 3 * gw), cur), pl.BlockSpec((None, blk, 3 * gw), prev)]
        out_shape += [jax.ShapeDtypeStruct((batch, seq // dil, dil * gw), BF16),
                      jax.ShapeDtypeStruct((batch, seq // dil, dil * gw), F32)]
        out_specs += [pl.BlockSpec((None, blk, gw), cur)] * 2
    outs = pl.pallas_call(
        functools.partial(_attn_kernel, blocks_per_subseq=tuple(nbs)),
        out_shape=out_shape,
        grid=(batch, n_blocks),
        in_specs=in_specs,
        out_specs=out_specs,
        compiler_params=pltpu.CompilerParams(dimension_semantics=("arbitrary", "arbitrary")),
        name="attn",
    )(*ins)
    return [o.reshape(-1, o.shape[-1]) for o in outs]


def _seg_sum(z, seg_ones):
    rows, c = z.shape
    n = c // LANES
    zs = jnp.concatenate([z[:, j * LANES:(j + 1) * LANES] for j in range(n)], axis=0).astype(BF16)
    out = _dot(zs, seg_ones)
    return jnp.concatenate([out[j * rows:(j + 1) * rows] for j in range(n)], axis=1)


def _wkv_kernel(rr_ref, rk_ref, rv_ref, rl_ref, w0_ref, w2_ref, a0_ref, a2_ref, g2_ref, kk_ref, ka_ref,
                rkp_ref, gnw_ref, gnb_ref, o_ref,
                state_ref, at_ref, rt_ref, bt_ref, kt_ref, v_ref, gam_ref, bonus_ref, g_ref):
    tt, c = rr_ref.shape
    n_pairs = c // LANES
    ck = WKV_CHUNK
    n_chunks = tt // ck
    staged = (at_ref, rt_ref, bt_ref, kt_ref, v_ref, gam_ref, bonus_ref, g_ref)

    @pl.when(pl.program_id(1) == 0)
    def _():
        state_ref[...] = jnp.zeros_like(state_ref)
        for ref in staged:
            ref[...] = jnp.zeros_like(ref)

    at_n, rt_n, bt_n, kt_n, v_n, gam_n, bonus_p, g_p = [ref[...] for ref in staged]

    lane = lax.broadcasted_iota(jnp.int32, (LANES, LANES), 1)
    rowi = lax.broadcasted_iota(jnp.int32, (LANES, LANES), 0)
    seg_ones = jnp.where(lane // HEAD_DIM == rowi // HEAD_DIM, 1.0, 0.0).astype(BF16)

    def stage_tile():
        r = rr_ref[...].astype(F32)
        k = rk_ref[...].astype(F32)
        v = rv_ref[...]
        lora = rl_ref[...]
        dwa = lora[:, 0:LANES]
        dg = lora[:, LANES:2 * LANES]
        zw = w0_ref[...] + _dot(jnp.tanh(dwa).astype(BF16), w2_ref[...])
        w_log = -(jnp.maximum(-zw, 0.0) + jnp.log(1.0 + jnp.exp(-jnp.abs(zw)))) - 0.5
        lw = -jnp.exp(w_log)
        a = _sigmoid(a0_ref[...] + _dot(dwa.astype(BF16), a2_ref[...]))
        g = _dot(_sigmoid(dg).astype(BF16), g2_ref[...])
        kk = k * kk_ref[...]
        kk = kk * lax.rsqrt(jnp.maximum(_seg_sum(kk * kk, seg_ones), 1e-24))
        kmod = k * (1.0 + (a - 1.0) * ka_ref[...])
        bonus = _seg_sum(r * kmod * rkp_ref[...], seg_ones) * v.astype(F32)
        tri = (lax.broadcasted_iota(jnp.int32, (ck, ck), 1)
               <= lax.broadcasted_iota(jnp.int32, (ck, ck), 0)).astype(BF16)
        hi, lo = _split2(lw)
        cum = jnp.concatenate([_dot(tri, hi[ci * ck:(ci + 1) * ck]) + _dot(tri, lo[ci * ck:(ci + 1) * ck])
                               for ci in range(n_chunks)], axis=0)
        e_in = jnp.exp(cum)
        e_neg = jnp.exp(-cum)
        return (
            (-kk * jnp.exp(cum - lw)).astype(BF16),
            (r * e_in).astype(BF16),
            (kk * a * e_neg).astype(BF16),
            (kmod * e_neg).astype(BF16),
            v,
            jnp.concatenate([e_in[(ci + 1) * ck - 8:(ci + 1) * ck] for ci in range(n_chunks)], axis=0),
            bonus,
            g,
        )

    t_idx = lax.broadcasted_iota(jnp.int32, (ck, LANES), 0)
    i_idx = lax.broadcasted_iota(jnp.int32, (ck, LANES), 1) % ck
    strict = i_idx < t_idx
    incl = i_idx <= t_idx
    lo_mask = lax.broadcasted_iota(jnp.int32, (1, LANES), 1) < HEAD_DIM
    eye_cat = jnp.where(i_idx == t_idx, 1.0, 0.0)
    bk_row_head = lax.broadcasted_iota(jnp.int32, (LANES, 2 * LANES), 0) // HEAD_DIM
    bk_col_head = (lax.broadcasted_iota(jnp.int32, (LANES, 2 * LANES), 1) // ck) % 2
    bk_mask = bk_row_head == bk_col_head

    def entries(t):
        return jnp.stack([t[ci * ck:(ci + 1) * ck, p * LANES:(p + 1) * LANES]
                          for ci in range(n_chunks) for p in range(n_pairs)])

    def stack_heads(t):
        z = jnp.zeros_like(t)
        return jnp.concatenate([jnp.where(lo_mask, t, z), jnp.where(lo_mask, z, t)], axis=1).astype(BF16)

    def bmm(a_, b_):
        return jnp.einsum('pmk,pkn->pmn', a_, b_, preferred_element_type=F32)

    at = entries(at_n)
    rt = entries(rt_n)
    bt = entries(bt_n)
    kt = entries(kt_n)
    bk_t = jnp.swapaxes(jnp.concatenate([bt, bt, kt, kt], axis=1), 1, 2)
    bk_t = jnp.where(bk_mask, bk_t, jnp.zeros_like(bk_t))
    vw = stack_heads(entries(v_n))
    gcol = jnp.swapaxes(
        jnp.stack([gam_n[ci * 8:(ci + 1) * 8, p * LANES:(p + 1) * LANES]
                   for ci in range(n_chunks) for p in range(n_pairs)]), 1, 2)[:, :, 7:8]

    gm = bmm(jnp.concatenate([at, rt], axis=1), bk_t)
    a_ab = jnp.where(strict, gm[:, :ck, :LANES], 0.0)
    a_ak = jnp.where(strict, gm[:, :ck, LANES:], 0.0).astype(BF16)
    a_rb = jnp.where(incl, gm[:, ck:, :LANES], 0.0).astype(BF16)
    a_rk = jnp.where(incl, gm[:, ck:, LANES:], 0.0).astype(BF16)
    new_staged = stage_tile()
    tm_ = eye_cat + a_ab
    pw = bmm(a_ab.astype(BF16), stack_heads(a_ab))
    for _ in range(4):
        both = bmm(jnp.concatenate([tm_, pw], axis=1).astype(BF16), stack_heads(pw))
        tm_ = tm_ + both[:, :ck, :]
        pw = both[:, ck:, :]
    tm_ = tm_ + bmm(tm_.astype(BF16), stack_heads(pw))
    t_cat = tm_.astype(BF16)
    by_v = bmm(jnp.concatenate([a_ak, a_rk, bk_t[:, :, LANES:]], axis=1), vw)
    ta = bmm(t_cat, stack_heads(at))
    tx = bmm(t_cat, stack_heads(by_v[:, :ck, :]))
    ra = rt + bmm(a_rb, stack_heads(ta))
    yc = by_v[:, ck:2 * ck, :] + bmm(a_rb, stack_heads(tx))
    dk = by_v[:, 2 * ck:, :]
    tara = jnp.concatenate([ta, ra], axis=1).astype(BF16)
    b_t = bk_t[:, :, :LANES]

    s = state_ref[...]
    y_rows = []
    for ci in range(n_chunks):
        es = slice(ci * n_pairs, (ci + 1) * n_pairs)
        o1 = bmm(tara[es], s.astype(BF16))
        u = o1[:, :ck, :] + tx[es]
        y = o1[:, ck:, :] + yc[es]
        y_rows.append(jnp.concatenate([y[p] for p in range(n_pairs)], axis=1))
        s = (s + bmm(b_t[es], stack_heads(u)) + dk[es]) * gcol[es]
    state_ref[...] = s

    y = jnp.concatenate(y_rows, axis=0)
    inv_n = 1.0 / HEAD_DIM
    mean = _seg_sum(y, seg_ones) * inv_n
    ycen = y - mean
    var = _seg_sum(ycen * ycen, seg_ones) * inv_n
    yn = ycen * lax.rsqrt(var + GN_EPS) * gnw_ref[...] + gnb_ref[...]
    o_ref[...] = ((yn + bonus_p) * g_p).astype(o_ref.dtype)

    for ref, val in zip(staged, new_staged):
        ref[...] = val


def _wkv(rr, rk, rv, rl, w0, w2p, a0, a2p, g2, k_k, k_a, r_k, gn_w, gn_b, *, batch, seq, tt):
    t_rows, c = rr.shape
    steps = seq // tt
    row_in = lambda b, j: (b * steps + jnp.minimum(j, steps - 1), 0)
    row_out = lambda b, j: (b * steps + jnp.maximum(j - 1, 0), 0)
    vec = _const_spec((1, c))
    n_gam = (tt // WKV_CHUNK) * 8
    return pl.pallas_call(
        _wkv_kernel,
        out_shape=jax.ShapeDtypeStruct((t_rows, c), BF16),
        grid=(batch, steps + 1),
        in_specs=[
            pl.BlockSpec((tt, c), row_in), pl.BlockSpec((tt, c), row_in), pl.BlockSpec((tt, c), row_in),
            pl.BlockSpec((tt, rl.shape[1]), row_in),
            vec, _const_spec(w2p.shape), vec, _const_spec(a2p.shape), _const_spec(g2.shape),
            vec, vec, vec, vec, vec,
        ],
        out_specs=pl.BlockSpec((tt, c), row_out),
        scratch_shapes=[pltpu.VMEM((c // LANES, LANES, LANES), F32)]
        + [pltpu.VMEM((tt, c), BF16)] * 5
        + [pltpu.VMEM((n_gam, c), F32), pltpu.VMEM((tt, c), F32), pltpu.VMEM((tt, c), F32)],
        compiler_params=pltpu.CompilerParams(
            dimension_semantics=("arbitrary", "arbitrary"), vmem_limit_bytes=VMEM_LIMIT),
        name="wkv",
    )(rr, rk, rv, rl, w0, w2p, a0, a2p, g2, k_k, k_a, r_k, gn_w, gn_b)


def _branch_kernel(o0_ref, l0_ref, o1_ref, l1_ref, o2_ref, l2_ref, gate_ref, orw_ref, x_ref,
                   wba_ref, wbr_ref, wo_ref, gt_ref, sc_ref, sh_ref, gpost_ref, gpre_ref,
                   x1_ref, h2_ref, nat_ref):
    tm, d = x_ref.shape
    gw = GROUP_WIDTH

    def natural(ref, dil, slab):
        if dil == 1:
            return ref[...].astype(F32)
        for r in range(dil):
            for s in range(gw // LANES):
                nat_ref[slab + s, pl.ds(r, tm // dil, stride=dil), :] = (
                    ref[:, r * gw + s * LANES:r * gw + (s + 1) * LANES].astype(F32))
        return jnp.concatenate([nat_ref[slab + s] for s in range(gw // LANES)], axis=1)

    dils = [dil for _, dil in ATTN_GROUPS]
    per_group = gw // LANES
    o0, o1, o2 = [natural(ref, dil, 2 * g * per_group)
                  for g, (ref, dil) in enumerate(zip((o0_ref, o1_ref, o2_ref), dils))]
    l0, l1, l2 = [natural(ref, dil, (2 * g + 1) * per_group)
                  for g, (ref, dil) in enumerate(zip((l0_ref, l1_ref, l2_ref), dils))]
    mx = jnp.maximum(jnp.maximum(l0, l1), l2)
    e0, e1, e2 = jnp.exp(l0 - mx), jnp.exp(l1 - mx), jnp.exp(l2 - mx)
    o_attn = (e0 * o0 + e1 * o1 + e2 * o2) / (e0 + e1 + e2)
    ya = _dot(o_attn.astype(BF16), wba_ref[...])
    yr = _dot(orw_ref[...], wbr_ref[...])
    merged = gate_ref[:, 0:d].astype(F32) * ya + gate_ref[:, d:2 * d].astype(F32) * yr
    mix = _dot(merged.astype(BF16), wo_ref[...])
    x1 = x_ref[...] + gt_ref[...] * (_rms(mix) * gpost_ref[...])
    x1_ref[...] = x1
    h2 = _rms(x1) * gpre_ref[...]
    h2_ref[...] = (h2 * (1.0 + sc_ref[...]) + sh_ref[...]).astype(BF16)


def _branch(attn_outs, gates, o_rwkv, x2, wba, wbr, wo, gt, sc, sh, gpost, gpre, *, seq, tm):
    t_rows, d = x2.shape
    tiles_per_seq = seq // tm
    row = lambda i: (i, 0)
    per_batch = lambda i: (i // tiles_per_seq, 0, 0)
    gw = GROUP_WIDTH
    attn_specs = [pl.BlockSpec((tm // dil, dil * gw), row) for _, dil in ATTN_GROUPS for _ in range(2)]
    n_slabs = 2 * len(ATTN_GROUPS) * (gw // LANES)
    return pl.pallas_call(
        _branch_kernel,
        out_shape=[jax.ShapeDtypeStruct((t_rows, d), F32), jax.ShapeDtypeStruct((t_rows, d), BF16)],
        grid=(t_rows // tm,),
        scratch_shapes=[pltpu.VMEM((n_slabs, tm, LANES), F32)],
        in_specs=attn_specs + [
            pl.BlockSpec((tm, 2 * d), row), pl.BlockSpec((tm, d), row), pl.BlockSpec((tm, d), row),
            _const_spec(wba.shape), _const_spec(wbr.shape), _const_spec(wo.shape),
            pl.BlockSpec((None, 1, d), per_batch), pl.BlockSpec((None, 1, d), per_batch),
            pl.BlockSpec((None, 1, d), per_batch),
            _const_spec((1, d)), _const_spec((1, d)),
        ],
        out_specs=[pl.BlockSpec((tm, d), row), pl.BlockSpec((tm, d), row)],
        compiler_params=pltpu.CompilerParams(
            dimension_semantics=("arbitrary",), vmem_limit_bytes=VMEM_LIMIT),
        name="branch",
    )(*attn_outs, gates, o_rwkv, x2, wba, wbr, wo, gt, sc, sh, gpost, gpre)


def _ffn_kernel(h_ref, x_ref, w1_ref, w2_ref, gt_ref, gpost_ref, o_ref, *, tf):
    h = h_ref[...]
    acc = jnp.zeros(x_ref.shape, F32)
    for j in range(w1_ref.shape[1] // tf):
        a = jnp.maximum(_dot(h, w1_ref[:, j * tf:(j + 1) * tf]), 0.0)
        acc = acc + _dot((a * a).astype(BF16), w2_ref[j * tf:(j + 1) * tf, :])
    o_ref[...] = x_ref[...] + gt_ref[...] * (_rms(acc) * gpost_ref[...])


def _ffn(h2, x1, w1, w2, gt, gpost, *, seq, tm, tf):
    t_rows, d = x1.shape
    tiles_per_seq = seq // tm
    row = lambda i: (i, 0)
    per_batch = lambda i: (i // tiles_per_seq, 0, 0)
    return pl.pallas_call(
        functools.partial(_ffn_kernel, tf=tf),
        out_shape=jax.ShapeDtypeStruct((t_rows, d), F32),
        grid=(t_rows // tm,),
        in_specs=[
            pl.BlockSpec((tm, d), row), pl.BlockSpec((tm, d), row),
            _const_spec(w1.shape), _const_spec(w2.shape),
            pl.BlockSpec((None, 1, d), per_batch), _const_spec((1, d)),
        ],
        out_specs=pl.BlockSpec((tm, d), row),
        compiler_params=pltpu.CompilerParams(
            dimension_semantics=("arbitrary",), vmem_limit_bytes=VMEM_LIMIT),
        name="ffn",
    )(h2, x1, w1, w2, gt, gpost)


def kernel(x, c, positions, ada_w, ada_b, norm_mix_pre, norm_mix_post, norm_ffn_pre, norm_ffn_post, w_in, shift_mu, decay_w0, decay_w2, iclr_a0, iclr_a2, gate_g2, k_k, k_a, r_k, gn_w, gn_b, w_branch, w_out, w_ff1, w_ff2):
    batch, seq, d = x.shape
    depth = ada_w.shape[0]
    t_rows = batch * seq
    n_groups = len(ATTN_GROUPS)
    attn_w = n_groups * GROUP_WIDTH
    attn_out_w = GROUP_WIDTH
    shift_w = shift_mu.shape[1]
    assert shift_w == 3 * d + DECAY_LORA + ICLR_LORA + GATE_LORA
    assert DECAY_LORA + ICLR_LORA == LANES and GATE_LORA == LANES
    tm = 512

    x2 = x.reshape(t_rows, d)
    posf = positions.astype(F32).reshape(t_rows, 1)
    half = ROPE_DIM // 2
    inv = ROPE_THETA ** (-jnp.arange(half, dtype=F32) * 2.0 / ROPE_DIM)
    inv_lane = jnp.tile(inv, LANES // half).reshape(1, LANES)
    c_pad = jnp.pad(c, ((0, 8 - batch), (0, 0)))
    vec = lambda p: p.reshape(1, -1)

    for l in range(depth):
        mod = _mod(c_pad, ada_w[l], ada_b[l].reshape(1, -1))[:batch]
        sh1, sc1, gt1, sh2, sc2, gt2 = [m.reshape(batch, 1, d) for m in jnp.split(mod, 6, axis=-1)]

        w = w_in[l].astype(BF16)
        wa = w[:, :3 * attn_w]
        wr = w[:, 3 * attn_w:3 * attn_w + shift_w]
        wg = w[:, 3 * attn_w + shift_w:]
        a0, a1, a2, rr, rk, rv, rl, gates = _inproj(
            x2, posf, sc1, sh1, vec(norm_mix_pre[l]), inv_lane, wa, wr, wg, vec(shift_mu[l]),
            seq=seq, tm=tm)

        attn_outs = _attn([a0, a1, a2], batch=batch, seq=seq)

        zeros = jnp.zeros((DECAY_LORA, d), BF16)
        w2p = jnp.concatenate([decay_w2[l].astype(BF16), zeros], axis=0)
        a2p = jnp.concatenate([zeros, iclr_a2[l].astype(BF16)], axis=0)
        o_rwkv = _wkv(rr, rk, rv, rl, vec(decay_w0[l]), w2p, vec(iclr_a0[l]), a2p,
                      gate_g2[l].astype(BF16), vec(k_k[l]), vec(k_a[l]), vec(r_k[l]),
                      vec(gn_w[l]), vec(gn_b[l]), batch=batch, seq=seq, tt=256)

        wb = w_branch[l].astype(BF16)
        x1, h2 = _branch(attn_outs, gates, o_rwkv, x2, wb[:attn_out_w], wb[attn_out_w:],
                         w_out[l].astype(BF16), gt1, sc2, sh2, vec(norm_mix_post[l]),
                         vec(norm_ffn_pre[l]), seq=seq, tm=tm)
        x2 = _ffn(h2, x1, w_ff1[l].astype(BF16), w_ff2[l].astype(BF16), gt2, vec(norm_ffn_post[l]),
                  seq=seq, tm=tm, tf=512)
    return x2.reshape(batch, seq, d)
```

```python
import functools

import jax
import jax.numpy as jnp
from jax import lax
from jax.experimental import pallas as pl
from jax.experimental.pallas import tpu as pltpu

F32 = jnp.float32
BF16 = jnp.bfloat16

HEAD_DIM = 64
ATTN_GROUPS = ((128, 1), (512, 4), (2048, 16))
ATTN_HEADS_PER_GROUP = 4
ATTN_BLOCK = 128
GROUP_WIDTH = ATTN_HEADS_PER_GROUP * HEAD_DIM
ROPE_THETA = 500000.0
ROPE_DIM = HEAD_DIM // 4
DECAY_LORA = 64
ICLR_LORA = 64
GATE_LORA = 128
NORM_EPS = 1e-6
GN_EPS = 64e-5
NEG_INF = -1e30

LANES = 128
WKV_CHUNK = 64
VMEM_LIMIT = 56 * 1024 * 1024


def _dot(a, b):
    return jnp.dot(a, b, preferred_element_type=F32)


def _dot_nt(a, b):
    return lax.dot_general(a, b, (((1,), (1,)), ((), ())), preferred_element_type=F32)


def _split2(x):
    hi = x.astype(BF16)
    lo = (x - hi.astype(F32)).astype(BF16)
    return hi, lo


def _sigmoid(x):
    return 1.0 / (1.0 + jnp.exp(-x))


def _rms(t):
    return t * lax.rsqrt(jnp.mean(t * t, axis=-1, keepdims=True) + NORM_EPS)


def _const_spec(shape):
    return pl.BlockSpec(shape, lambda *_: (0,) * len(shape), pipeline_mode=pl.Buffered(1))


def _mod_kernel(c_ref, w_ref, b_ref, o_ref):
    c = c_ref[...]
    a = c * _sigmoid(c)
    ah, al = _split2(a)
    wh, wl = _split2(w_ref[...])
    o_ref[...] = _dot(ah, wh) + _dot(ah, wl) + _dot(al, wh) + b_ref[...]


def _mod(c_pad, ada_w, ada_b):
    rows, d = c_pad.shape
    n = ada_w.shape[1]
    tn = 768
    return pl.pallas_call(
        _mod_kernel,
        out_shape=jax.ShapeDtypeStruct((rows, n), F32),
        grid=(n // tn,),
        in_specs=[
            pl.BlockSpec((rows, d), lambda j: (0, 0)),
            pl.BlockSpec((d, tn), lambda j: (0, j)),
            pl.BlockSpec((1, tn), lambda j: (0, j)),
        ],
        out_specs=pl.BlockSpec((rows, tn), lambda j: (0, j)),
        name="mod",
    )(c_pad, ada_w, ada_b)


def _inproj_kernel(x_ref, pos_ref, sc_ref, sh_ref, g_ref, inv_ref, spread_ref, wa_ref, wr_ref, wg_ref, mu_ref,
                   a0_ref, a1_ref, a2_ref, rr_ref, rk_ref, rv_ref, rl_ref, gate_ref, carry_ref, dil_ref,
                   *, tiles_per_seq):
    i = pl.program_id(0)
    tm = x_ref.shape[0]
    d = x_ref.shape[1]
    h = _rms(x_ref[...]) * g_ref[...]
    hb = (h * (1.0 + sc_ref[...]) + sh_ref[...]).astype(BF16)

    half = ROPE_DIM // 2
    ang = inv_ref[...] * pos_ref[...]
    cs_t = jnp.concatenate([jnp.cos(ang), jnp.sin(ang)], axis=0).T
    tab = _dot(jnp.concatenate(_split2(cs_t), axis=1), spread_ref[...])
    lane = lax.broadcasted_iota(jnp.int32, (1, LANES), 1) % HEAD_DIM
    cc = tab[:, 0:LANES] + jnp.where(lane < ROPE_DIM, 0.0, 1.0)
    s_lo = tab[:, LANES:2 * LANES]
    s_hi = tab[:, 2 * LANES:3 * LANES]

    def rope(t):
        return t * cc + pltpu.roll(t, LANES - half, 1) * s_lo + pltpu.roll(t, half, 1) * s_hi

    a_refs = (a0_ref, a1_ref, a2_ref)
    aw = len(ATTN_GROUPS) * GROUP_WIDTH
    gw = GROUP_WIDTH
    slab = 0
    for which in range(3):
        t = _dot(hb, wa_ref[:, which * aw:(which + 1) * aw])
        for g, (_, dil) in enumerate(ATTN_GROUPS):
            halves = [t[:, g * gw + s * LANES:g * gw + (s + 1) * LANES] for s in range(gw // LANES)]
            if which < 2:
                halves = [rope(hv) for hv in halves]
            if which == 0:
                halves = [hv * (HEAD_DIM ** -0.5) for hv in halves]
            for s, hv in enumerate(halves):
                if dil == 1:
                    a_refs[g][:, which * gw + s * LANES:which * gw + (s + 1) * LANES] = hv.astype(BF16)
                    continue
                dil_ref[slab] = hv
                for r in range(dil):
                    col = r * 3 * gw + which * gw + s * LANES
                    a_refs[g][:, col:col + LANES] = dil_ref[slab, pl.ds(r, tm // dil, stride=dil), :].astype(BF16)
                slab += 1

    first = (i % tiles_per_seq) == 0
    row0 = lax.broadcasted_iota(jnp.int32, (tm, 1), 0) == 0
    col = 0
    for ref in (rr_ref, rk_ref, rv_ref, rl_ref):
        n = ref.shape[1]
        u = _dot(hb, wr_ref[:, col:col + n])
        carry = jnp.where(first, 0.0, carry_ref[:, col:col + n])
        prev = jnp.where(row0, carry, pltpu.roll(u, 1, 0))
        carry_ref[:, col:col + n] = u[tm - 1:tm, :]
        ref[...] = (u + (prev - u) * mu_ref[:, col:col + n]).astype(ref.dtype)
        col += n

    for j in range(2):
        gj = _dot(hb, wg_ref[:, j * d:(j + 1) * d])
        gate_ref[:, j * d:(j + 1) * d] = _sigmoid(gj).astype(BF16)


def _rope_spread():
    half = ROPE_DIM // 2
    cols = jnp.arange(3 * LANES)
    block, in_head = cols // LANES, cols % HEAD_DIM
    freq = in_head % half
    src_row = jnp.where(block == 0, freq, half + freq)
    sign = jnp.where(block == 1, -1.0, 1.0)
    used = jnp.where(block == 0, in_head < ROPE_DIM,
                     jnp.where(block == 1, in_head < half, (in_head >= half) & (in_head < ROPE_DIM)))
    one_piece = jnp.where((jnp.arange(2 * half)[:, None] == src_row[None, :]) & used[None, :], sign[None, :], 0.0)
    return jnp.concatenate([one_piece, one_piece], axis=0).astype(BF16)


def _inproj(x2, pos_rows, sc, sh, gain, inv_col, wa, wr, wg, mu, *, seq, tm):
    t_rows, d = x2.shape
    tiles_per_seq = seq // tm
    aw = wa.shape[1] // 3
    shift_w = wr.shape[1]
    lora_w = shift_w - 3 * d
    row = lambda i: (i, 0)
    per_batch = lambda i: (i // tiles_per_seq, 0, 0)
    dils = [dil for _, dil in ATTN_GROUPS]
    out_shape = (
        [jax.ShapeDtypeStruct((t_rows // dil, dil * aw), BF16) for dil in dils]
        + [jax.ShapeDtypeStruct((t_rows, d), BF16)] * 3
        + [jax.ShapeDtypeStruct((t_rows, lora_w), F32), jax.ShapeDtypeStruct((t_rows, 2 * d), BF16)]
    )
    out_specs = (
        [pl.BlockSpec((tm // dil, dil * aw), row) for dil in dils]
        + [pl.BlockSpec((tm, d), row)] * 3
        + [pl.BlockSpec((tm, lora_w), row), pl.BlockSpec((tm, 2 * d), row)]
    )
    n_slabs = sum(3 * GROUP_WIDTH // LANES for dil in dils if dil > 1)
    spread = _rope_spread()
    return pl.pallas_call(
        functools.partial(_inproj_kernel, tiles_per_seq=tiles_per_seq),
        out_shape=out_shape,
        grid=(t_rows // tm,),
        in_specs=[
            pl.BlockSpec((tm, d), row),
            pl.BlockSpec((None, 1, tm), lambda i: (i, 0, 0)),
            pl.BlockSpec((None, 1, d), per_batch),
            pl.BlockSpec((None, 1, d), per_batch),
            _const_spec((1, d)),
            _const_spec(inv_col.shape),
            _const_spec(spread.shape),
            _const_spec(wa.shape),
            _const_spec(wr.shape),
            _const_spec(wg.shape),
            _const_spec((1, shift_w)),
        ],
        out_specs=out_specs,
        scratch_shapes=[pltpu.VMEM((1, shift_w), F32), pltpu.VMEM((n_slabs, tm, LANES), F32)],
        compiler_params=pltpu.CompilerParams(
            dimension_semantics=("arbitrary",), vmem_limit_bytes=VMEM_LIMIT),
        name="inproj",
    )(x2, pos_rows, sc, sh, gain, inv_col, spread, wa, wr, wg, mu)


def _attn_kernel(*refs, blocks_per_subseq):
    n_groups = len(blocks_per_subseq)
    in_refs, out_refs = refs[:2 * n_groups], refs[2 * n_groups:]
    m = pl.program_id(1)
    blk = ATTN_BLOCK
    gw = GROUP_WIDTH
    nh = ATTN_HEADS_PER_GROUP
    row = lax.broadcasted_iota(jnp.int32, (nh * blk, blk), 0) % blk
    colk = lax.broadcasted_iota(jnp.int32, (nh * blk, blk), 1)
    mask_cur = colk <= row
    mask_prev_band = colk >= row
    head_of_lane = lax.broadcasted_iota(jnp.int32, (1, gw), 1) // HEAD_DIM
    groups = range(n_groups)
    cur = [in_refs[2 * g] for g in groups]
    prev = [in_refs[2 * g + 1] for g in groups]
    q4, sc, sp = [], [], []
    for g in groups:
        q = cur[g][:, 0:gw]
        zq = jnp.zeros_like(q)
        q4.append(jnp.concatenate([jnp.where(head_of_lane == hd, q, zq) for hd in range(nh)], axis=0))
    for g in groups:
        mask_prev = mask_prev_band & ((m % blocks_per_subseq[g]) > 0)
        sc.append(jnp.where(mask_cur, _dot_nt(q4[g], cur[g][:, gw:2 * gw]), NEG_INF))
        sp.append(jnp.where(mask_prev, _dot_nt(q4[g], prev[g][:, gw:2 * gw]), NEG_INF))
    mx = [jnp.max(jnp.maximum(sc[g], sp[g]), axis=-1, keepdims=True) for g in groups]
    pc = [jnp.exp(sc[g] - mx[g]) for g in groups]
    pp = [jnp.exp(sp[g] - mx[g]) for g in groups]
    den = [jnp.sum(pc[g] + pp[g], axis=-1, keepdims=True) for g in groups]
    pv = [_dot(pc[g].astype(BF16), cur[g][:, 2 * gw:3 * gw]) + _dot(pp[g].astype(BF16), prev[g][:, 2 * gw:3 * gw])
          for g in groups]
    for g in groups:
        o4 = pv[g] * (1.0 / den[g])
        l4 = mx[g] + jnp.log(den[g])
        o_acc = o4[0:blk]
        l_acc = jnp.broadcast_to(l4[0:blk], (blk, gw))
        for hd in range(1, nh):
            hm = head_of_lane == hd
            o_acc = jnp.where(hm, o4[hd * blk:(hd + 1) * blk], o_acc)
            l_acc = jnp.where(hm, l4[hd * blk:(hd + 1) * blk], l_acc)
        out_refs[2 * g][...] = o_acc.astype(BF16)
        out_refs[2 * g + 1][...] = l_acc


def _attn(a_list, *, batch, seq):
    assert all(w // dil == ATTN_BLOCK for w, dil in ATTN_GROUPS)
    blk, gw = ATTN_BLOCK, GROUP_WIDTH
    n_blocks = seq // blk
    ins, in_specs, out_shape, out_specs, nbs = [], [], [], [], []
    for a, (_, dil) in zip(a_list, ATTN_GROUPS):
        assert seq % (dil * blk) == 0
        nb = n_blocks // dil
        nbs.append(nb)
        view = a.reshape(batch, seq // dil, dil * 3 * gw)
        cur = lambda b, m, nb=nb: (b, m % nb, m // nb)
        prev = lambda b, m, nb=nb: (b, jnp.maximum(m % nb - 1, 0), m // nb)
        ins += [view, view]
        in_specs += [pl.BlockSpec((None, blk, 3 * gw), cur), pl.BlockSpec((None, blk, 3 * gw), prev)]
        out_shape += [jax.ShapeDtypeStruct((batch, seq // dil, dil * gw), BF16),
                      jax.ShapeDtypeStruct((batch, seq // dil, dil * gw), F32)]
        out_specs += [pl.BlockSpec((None, blk, gw), cur)] * 2
    outs = pl.pallas_call(
        functools.partial(_attn_kernel, blocks_per_subseq=tuple(nbs)),
        out_shape=out_shape,
        grid=(batch, n_blocks),
        in_specs=in_specs,
        out_specs=out_specs,
        compiler_params=pltpu.CompilerParams(dimension_semantics=("arbitrary", "arbitrary")),
        name="attn",
    )(*ins)
    return [o.reshape(-1, o.shape[-1]) for o in outs]


def _seg_sum(z, seg_ones):
    rows, c = z.shape
    n = c // LANES
    zs = jnp.concatenate([z[:, j * LANES:(j + 1) * LANES] for j in range(n)], axis=0).astype(BF16)
    out = _dot(zs, seg_ones)
    return jnp.concatenate([out[j * rows:(j + 1) * rows] for j in range(n)], axis=1)


def _wkv_kernel(rr_ref, rk_ref, rv_ref, rl_ref, w0_ref, w2_ref, a0_ref, a2_ref, g2_ref, kk_ref, ka_ref,
                rkp_ref, gnw_ref, gnb_ref, o_ref,
                state_ref, at_ref, rt_ref, bt_ref, kt_ref, v_ref, gam_ref, bonus_ref, g_ref):
    tt, c = rr_ref.shape
    n_pairs = c // LANES
    ck = WKV_CHUNK
    n_chunks = tt // ck
    staged = (at_ref, rt_ref, bt_ref, kt_ref, v_ref, gam_ref, bonus_ref, g_ref)

    @pl.when(pl.program_id(1) == 0)
    def _():
        state_ref[...] = jnp.zeros_like(state_ref)
        for ref in staged:
            ref[...] = jnp.zeros_like(ref)

    at_n, rt_n, bt_n, kt_n, v_n, gam_n, bonus_p, g_p = [ref[...] for ref in staged]

    lane = lax.broadcasted_iota(jnp.int32, (LANES, LANES), 1)
    rowi = lax.broadcasted_iota(jnp.int32, (LANES, LANES), 0)
    seg_ones = jnp.where(lane // HEAD_DIM == rowi // HEAD_DIM, 1.0, 0.0).astype(BF16)

    def stage_lora():
        lora = rl_ref[...]
        dwa = lora[:, 0:LANES]
        dg = lora[:, LANES:2 * LANES]
        zw = w0_ref[...] + _dot(jnp.tanh(dwa).astype(BF16), w2_ref[...])
        za = a0_ref[...] + _dot(dwa.astype(BF16), a2_ref[...])
        return zw, za, _dot(_sigmoid(dg).astype(BF16), g2_ref[...])

    tri = (lax.broadcasted_iota(jnp.int32, (ck, ck), 1)
           <= lax.broadcasted_iota(jnp.int32, (ck, ck), 0)).astype(BF16)

    def stage_chunk(ci, zw, za):
        rows = slice(ci * ck, (ci + 1) * ck)
        r = rr_ref[rows, :].astype(F32)
        k = rk_ref[rows, :].astype(F32)
        v = rv_ref[rows, :]
        zw = zw[rows]
        w_log = -(jnp.maximum(-zw, 0.0) + jnp.log(1.0 + jnp.exp(-jnp.abs(zw)))) - 0.5
        lw = -jnp.exp(w_log)
        a = _sigmoid(za[rows])
        kk = k * kk_ref[...]
        kk = kk * lax.rsqrt(jnp.maximum(_seg_sum(kk * kk, seg_ones), 1e-24))
        kmod = k * (1.0 + (a - 1.0) * ka_ref[...])
        bonus = _seg_sum(r * kmod * rkp_ref[...], seg_ones) * v.astype(F32)
        hi, lo = _split2(lw)
        cum = _dot(tri, hi) + _dot(tri, lo)
        e_in = jnp.exp(cum)
        e_neg = jnp.exp(-cum)
        return (
            (-kk * jnp.exp(cum - lw)).astype(BF16),
            (r * e_in).astype(BF16),
            (kk * a * e_neg).astype(BF16),
            (kmod * e_neg).astype(BF16),
            v,
            e_in[ck - 8:ck],
            bonus,
        )

    t_idx = lax.broadcasted_iota(jnp.int32, (ck, LANES), 0)
    i_idx = lax.broadcasted_iota(jnp.int32, (ck, LANES), 1) % ck
    strict = i_idx < t_idx
    incl = i_idx <= t_idx
    lo_mask = lax.broadcasted_iota(jnp.int32, (1, LANES), 1) < HEAD_DIM
    eye_cat = jnp.where(i_idx == t_idx, 1.0, 0.0)
    bk_row_head = lax.broadcasted_iota(jnp.int32, (LANES, 2 * LANES), 0) // HEAD_DIM
    bk_col_head = (lax.broadcasted_iota(jnp.int32, (LANES, 2 * LANES), 1) // ck) % 2
    bk_mask = bk_row_head == bk_col_head

    def entries(t):
        return jnp.stack([t[ci * ck:(ci + 1) * ck, p * LANES:(p + 1) * LANES]
                          for ci in range(n_chunks) for p in range(n_pairs)])

    def stack_heads(t):
        z = jnp.zeros_like(t)
        return jnp.concatenate([jnp.where(lo_mask, t, z), jnp.where(lo_mask, z, t)], axis=1).astype(BF16)

    def bmm(a_, b_):
        return jnp.einsum('pmk,pkn->pmn', a_, b_, preferred_element_type=F32)

    at = entries(at_n)
    rt = entries(rt_n)
    bt = entries(bt_n)
    kt = entries(kt_n)
    bk_t = jnp.swapaxes(jnp.concatenate([bt, bt, kt, kt], axis=1), 1, 2)
    bk_t = jnp.where(bk_mask, bk_t, jnp.zeros_like(bk_t))
    vw = stack_heads(entries(v_n))
    gcol = jnp.swapaxes(
        jnp.stack([gam_n[ci * 8:(ci + 1) * 8, p * LANES:(p + 1) * LANES]
                   for ci in range(n_chunks) for p in range(n_pairs)]), 1, 2)[:, :, 7:8]

    gm = bmm(jnp.concatenate([at, rt], axis=1), bk_t)
    a_ab = jnp.where(strict, gm[:, :ck, :LANES], 0.0)
    a_ak = jnp.where(strict, gm[:, :ck, LANES:], 0.0).astype(BF16)
    a_rb = jnp.where(incl, gm[:, ck:, :LANES], 0.0).astype(BF16)
    a_rk = jnp.where(incl, gm[:, ck:, LANES:], 0.0).astype(BF16)
    zw_t, za_t, g_new = stage_lora()
    new_chunks = []
    stage_order = iter(range(n_chunks))

    def stage_next():
        ci = next(stage_order, None)
        if ci is not None:
            new_chunks.append(stage_chunk(ci, zw_t, za_t))

    stage_next()
    tm_ = eye_cat + a_ab
    pw = bmm(a_ab.astype(BF16), stack_heads(a_ab))
    for step in range(4):
        both = bmm(jnp.concatenate([tm_, pw], axis=1).astype(BF16), stack_heads(pw))
        tm_ = tm_ + both[:, :ck, :]
        pw = both[:, ck:, :]
        if step % 2 == 0:
            stage_next()
    tm_ = tm_ + bmm(tm_.astype(BF16), stack_heads(pw))
    t_cat = tm_.astype(BF16)
    by_v = bmm(jnp.concatenate([a_ak, a_rk, bk_t[:, :, LANES:]], axis=1), vw)
    ta = bmm(t_cat, stack_heads(at))
    for _ in range(n_chunks):
        stage_next()
    tx = bmm(t_cat, stack_heads(by_v[:, :ck, :]))
    ra = rt + bmm(a_rb, stack_heads(ta))
    yc = by_v[:, ck:2 * ck, :] + bmm(a_rb, stack_heads(tx))
    dk = by_v[:, 2 * ck:, :]
    new_staged = [jnp.concatenate(parts, axis=0) for parts in zip(*new_chunks)] + [g_new]
    tara = jnp.concatenate([ta, ra], axis=1).astype(BF16)
    b_t = bk_t[:, :, :LANES]

    s = state_ref[...]
    y_rows = []
    for ci in range(n_chunks):
        es = slice(ci * n_pairs, (ci + 1) * n_pairs)
        o1 = bmm(tara[es], s.astype(BF16))
        u = o1[:, :ck, :] + tx[es]
        y = o1[:, ck:, :] + yc[es]
        y_rows.append(jnp.concatenate([y[p] for p in range(n_pairs)], axis=1))
        s = (s + bmm(b_t[es], stack_heads(u)) + dk[es]) * gcol[es]
    state_ref[...] = s

    y = jnp.concatenate(y_rows, axis=0)
    inv_n = 1.0 / HEAD_DIM
    mean = _seg_sum(y, seg_ones) * inv_n
    ycen = y - mean
    var = _seg_sum(ycen * ycen, seg_ones) * inv_n
    yn = ycen * lax.rsqrt(var + GN_EPS) * gnw_ref[...] + gnb_ref[...]
    o_ref[...] = ((yn + bonus_p) * g_p).astype(o_ref.dtype)

    for ref, val in zip(staged, new_staged):
        ref[...] = val


def _wkv(rr, rk, rv, rl, w0, w2p, a0, a2p, g2, k_k, k_a, r_k, gn_w, gn_b, *, batch, seq, tt):
    t_rows, c = rr.shape
    steps = seq // tt
    row_in = lambda b, j: (b * steps + jnp.minimum(j, steps - 1), 0)
    row_out = lambda b, j: (b * steps + jnp.maximum(j - 1, 0), 0)
    vec = _const_spec((1, c))
    n_gam = (tt // WKV_CHUNK) * 8
    return pl.pallas_call(
        _wkv_kernel,
        out_shape=jax.ShapeDtypeStruct((t_rows, c), BF16),
        grid=(batch, steps + 1),
        in_specs=[
            pl.BlockSpec((tt, c), row_in), pl.BlockSpec((tt, c), row_in), pl.BlockSpec((tt, c), row_in),
            pl.BlockSpec((tt, rl.shape[1]), row_in),
            vec, _const_spec(w2p.shape), vec, _const_spec(a2p.shape), _const_spec(g2.shape),
            vec, vec, vec, vec, vec,
        ],
        out_specs=pl.BlockSpec((tt, c), row_out),
        scratch_shapes=[pltpu.VMEM((c // LANES, LANES, LANES), F32)]
        + [pltpu.VMEM((tt, c), BF16)] * 5
        + [pltpu.VMEM((n_gam, c), F32), pltpu.VMEM((tt, c), F32), pltpu.VMEM((tt, c), F32)],
        compiler_params=pltpu.CompilerParams(
            dimension_semantics=("arbitrary", "arbitrary"), vmem_limit_bytes=VMEM_LIMIT),
        name="wkv",
    )(rr, rk, rv, rl, w0, w2p, a0, a2p, g2, k_k, k_a, r_k, gn_w, gn_b)


def _branch_kernel(o0_ref, l0_ref, o1_ref, l1_ref, o2_ref, l2_ref, gate_ref, orw_ref, x_ref,
                   wba_ref, wbr_ref, wo_ref, gt_ref, sc_ref, sh_ref, gpost_ref, gpre_ref,
                   x1_ref, h2_ref, nat_ref):
    tm, d = x_ref.shape
    gw = GROUP_WIDTH

    def natural(ref, dil, slab):
        if dil == 1:
            return ref[...].astype(F32)
        for r in range(dil):
            for s in range(gw // LANES):
                nat_ref[slab + s, pl.ds(r, tm // dil, stride=dil), :] = (
                    ref[:, r * gw + s * LANES:r * gw + (s + 1) * LANES].astype(F32))
        return jnp.concatenate([nat_ref[slab + s] for s in range(gw // LANES)], axis=1)

    dils = [dil for _, dil in ATTN_GROUPS]
    per_group = gw // LANES
    o0, o1, o2 = [natural(ref, dil, 2 * g * per_group)
                  for g, (ref, dil) in enumerate(zip((o0_ref, o1_ref, o2_ref), dils))]
    l0, l1, l2 = [natural(ref, dil, (2 * g + 1) * per_group)
                  for g, (ref, dil) in enumerate(zip((l0_ref, l1_ref, l2_ref), dils))]
    mx = jnp.maximum(jnp.maximum(l0, l1), l2)
    e0, e1, e2 = jnp.exp(l0 - mx), jnp.exp(l1 - mx), jnp.exp(l2 - mx)
    o_attn = (e0 * o0 + e1 * o1 + e2 * o2) / (e0 + e1 + e2)
    ya = _dot(o_attn.astype(BF16), wba_ref[...])
    yr = _dot(orw_ref[...], wbr_ref[...])
    merged = gate_ref[:, 0:d].astype(F32) * ya + gate_ref[:, d:2 * d].astype(F32) * yr
    mix = _dot(merged.astype(BF16), wo_ref[...])
    x1 = x_ref[...] + gt_ref[...] * (_rms(mix) * gpost_ref[...])
    x1_ref[...] = x1
    h2 = _rms(x1) * gpre_ref[...]
    h2_ref[...] = (h2 * (1.0 + sc_ref[...]) + sh_ref[...]).astype(BF16)


def _branch(attn_outs, gates, o_rwkv, x2, wba, wbr, wo, gt, sc, sh, gpost, gpre, *, seq, tm):
    t_rows, d = x2.shape
    tiles_per_seq = seq // tm
    row = lambda i: (i, 0)
    per_batch = lambda i: (i // tiles_per_seq, 0, 0)
    gw = GROUP_WIDTH
    attn_specs = [pl.BlockSpec((tm // dil, dil * gw), row) for _, dil in ATTN_GROUPS for _ in range(2)]
    n_slabs = 2 * len(ATTN_GROUPS) * (gw // LANES)
    return pl.pallas_call(
        _branch_kernel,
        out_shape=[jax.ShapeDtypeStruct((t_rows, d), F32), jax.ShapeDtypeStruct((t_rows, d), BF16)],
        grid=(t_rows // tm,),
        scratch_shapes=[pltpu.VMEM((n_slabs, tm, LANES), F32)],
        in_specs=attn_specs + [
            pl.BlockSpec((tm, 2 * d), row), pl.BlockSpec((tm, d), row), pl.BlockSpec((tm, d), row),
            _const_spec(wba.shape), _const_spec(wbr.shape), _const_spec(wo.shape),
            pl.BlockSpec((None, 1, d), per_batch), pl.BlockSpec((None, 1, d), per_batch),
            pl.BlockSpec((None, 1, d), per_batch),
            _const_spec((1, d)), _const_spec((1, d)),
        ],
        out_specs=[pl.BlockSpec((tm, d), row), pl.BlockSpec((tm, d), row)],
        compiler_params=pltpu.CompilerParams(
            dimension_semantics=("arbitrary",), vmem_limit_bytes=VMEM_LIMIT),
        name="branch",
    )(*attn_outs, gates, o_rwkv, x2, wba, wbr, wo, gt, sc, sh, gpost, gpre)


def _ffn_kernel(h_ref, x_ref, w1_ref, w2_ref, gt_ref, gpost_ref, o_ref, *, tf):
    h = h_ref[...]
    acc = jnp.zeros(x_ref.shape, F32)
    for j in range(w1_ref.shape[1] // tf):
        a = jnp.maximum(_dot(h, w1_ref[:, j * tf:(j + 1) * tf]), 0.0)
        acc = acc + _dot((a * a).astype(BF16), w2_ref[j * tf:(j + 1) * tf, :])
    o_ref[...] = x_ref[...] + gt_ref[...] * (_rms(acc) * gpost_ref[...])


def _ffn(h2, x1, w1, w2, gt, gpost, *, seq, tm, tf):
    t_rows, d = x1.shape
    tiles_per_seq = seq // tm
    row = lambda i: (i, 0)
    per_batch = lambda i: (i // tiles_per_seq, 0, 0)
    return pl.pallas_call(
        functools.partial(_ffn_kernel, tf=tf),
        out_shape=jax.ShapeDtypeStruct((t_rows, d), F32),
        grid=(t_rows // tm,),
        in_specs=[
            pl.BlockSpec((tm, d), row), pl.BlockSpec((tm, d), row),
            _const_spec(w1.shape), _const_spec(w2.shape),
            pl.BlockSpec((None, 1, d), per_batch), _const_spec((1, d)),
        ],
        out_specs=pl.BlockSpec((tm, d), row),
        compiler_params=pltpu.CompilerParams(
            dimension_semantics=("arbitrary",), vmem_limit_bytes=VMEM_LIMIT),
        name="ffn",
    )(h2, x1, w1, w2, gt, gpost)


def kernel(x, c, positions, ada_w, ada_b, norm_mix_pre, norm_mix_post, norm_ffn_pre, norm_ffn_post, w_in, shift_mu, decay_w0, decay_w2, iclr_a0, iclr_a2, gate_g2, k_k, k_a, r_k, gn_w, gn_b, w_branch, w_out, w_ff1, w_ff2):
    batch, seq, d = x.shape
    depth = ada_w.shape[0]
    t_rows = batch * seq
    n_groups = len(ATTN_GROUPS)
    attn_w = n_groups * GROUP_WIDTH
    attn_out_w = GROUP_WIDTH
    shift_w = shift_mu.shape[1]
    assert shift_w == 3 * d + DECAY_LORA + ICLR_LORA + GATE_LORA
    assert DECAY_LORA + ICLR_LORA == LANES and GATE_LORA == LANES
    tm = 512

    x2 = x.reshape(t_rows, d)
    pos_rows = positions.astype(F32).reshape(t_rows // tm, 1, tm)
    half = ROPE_DIM // 2
    inv_col = (ROPE_THETA ** (-jnp.arange(half, dtype=F32) * 2.0 / ROPE_DIM)).reshape(half, 1)
    c_pad = jnp.pad(c, ((0, 8 - batch), (0, 0)))
    vec = lambda p: p.reshape(1, -1)

    for l in range(depth):
        mod = _mod(c_pad, ada_w[l], ada_b[l].reshape(1, -1))[:batch]
        sh1, sc1, gt1, sh2, sc2, gt2 = [m.reshape(batch, 1, d) for m in jnp.split(mod, 6, axis=-1)]

        w = w_in[l].astype(BF16)
        wa = w[:, :3 * attn_w]
        wr = w[:, 3 * attn_w:3 * attn_w + shift_w]
        wg = w[:, 3 * attn_w + shift_w:]
        a0, a1, a2, rr, rk, rv, rl, gates = _inproj(
            x2, pos_rows, sc1, sh1, vec(norm_mix_pre[l]), inv_col, wa, wr, wg, vec(shift_mu[l]),
            seq=seq, tm=tm)

        attn_outs = _attn([a0, a1, a2], batch=batch, seq=seq)

        zeros = jnp.zeros((DECAY_LORA, d), BF16)
        w2p = jnp.concatenate([decay_w2[l].astype(BF16), zeros], axis=0)
        a2p = jnp.concatenate([zeros, iclr_a2[l].astype(BF16)], axis=0)
        o_rwkv = _wkv(rr, rk, rv, rl, vec(decay_w0[l]), w2p, vec(iclr_a0[l]), a2p,
                      gate_g2[l].astype(BF16), vec(k_k[l]), vec(k_a[l]), vec(r_k[l]),
                      vec(gn_w[l]), vec(gn_b[l]), batch=batch, seq=seq, tt=256)

        wb = w_branch[l].astype(BF16)
        x1, h2 = _branch(attn_outs, gates, o_rwkv, x2, wb[:attn_out_w], wb[attn_out_w:],
                         w_out[l].astype(BF16), gt1, sc2, sh2, vec(norm_mix_post[l]),
                         vec(norm_ffn_pre[l]), seq=seq, tm=tm)
        x2 = _ffn(h2, x1, w_ff1[l].astype(BF16), w_ff2[l].astype(BF16), gt2, vec(norm_ffn_post[l]),
                  seq=seq, tm=tm, tf=512)
    return x2.reshape(batch, seq, d)
```

```python
import functools

import jax
import jax.numpy as jnp
from jax import lax
from jax.experimental import pallas as pl
from jax.experimental.pallas import tpu as pltpu

F32 = jnp.float32
BF16 = jnp.bfloat16

HEAD_DIM = 64
ATTN_GROUPS = ((128, 1), (512, 4), (2048, 16))
ATTN_HEADS_PER_GROUP = 4
ATTN_BLOCK = 128
GROUP_WIDTH = ATTN_HEADS_PER_GROUP * HEAD_DIM
ROPE_THETA = 500000.0
ROPE_DIM = HEAD_DIM // 4
DECAY_LORA = 64
ICLR_LORA = 64
GATE_LORA = 128
NORM_EPS = 1e-6
GN_EPS = 64e-5
NEG_INF = -1e30

LANES = 128
WKV_CHUNK = 64
VMEM_LIMIT = 56 * 1024 * 1024


def _dot(a, b):
    return jnp.dot(a, b, preferred_element_type=F32)


def _dot_nt(a, b):
    return lax.dot_general(a, b, (((1,), (1,)), ((), ())), preferred_element_type=F32)


def _split2(x):
    hi = x.astype(BF16)
    lo = (x - hi.astype(F32)).astype(BF16)
    return hi, lo


def _sigmoid(x):
    return 1.0 / (1.0 + jnp.exp(-x))


def _rms(t):
    return t * lax.rsqrt(jnp.mean(t * t, axis=-1, keepdims=True) + NORM_EPS)


def _const_spec(shape):
    return pl.BlockSpec(shape, lambda *_: (0,) * len(shape), pipeline_mode=pl.Buffered(1))


def _mod_kernel(c_ref, w_ref, b_ref, o_ref):
    c = c_ref[...]
    a = c * _sigmoid(c)
    ah, al = _split2(a)
    wh, wl = _split2(w_ref[...])
    o_ref[...] = _dot(ah, wh) + _dot(ah, wl) + _dot(al, wh) + b_ref[...]


def _mod(c_pad, ada_w, ada_b):
    rows, d = c_pad.shape
    n = ada_w.shape[1]
    tn = 768
    return pl.pallas_call(
        _mod_kernel,
        out_shape=jax.ShapeDtypeStruct((rows, n), F32),
        grid=(n // tn,),
        in_specs=[
            pl.BlockSpec((rows, d), lambda j: (0, 0)),
            pl.BlockSpec((d, tn), lambda j: (0, j)),
            pl.BlockSpec((1, tn), lambda j: (0, j)),
        ],
        out_specs=pl.BlockSpec((rows, tn), lambda j: (0, j)),
        name="mod",
    )(c_pad, ada_w, ada_b)


def _inproj_kernel(x_ref, pos_ref, sc_ref, sh_ref, g_ref, inv_ref, spread_ref, w_ref, mu_ref,
                   a0_ref, a1_ref, a2_ref, rr_ref, rk_ref, rv_ref, rl_ref, gate_ref, carry_ref, dil_ref,
                   *, tiles_per_seq):
    i = pl.program_id(0)
    tm = x_ref.shape[0]
    d = x_ref.shape[1]
    h = _rms(x_ref[...]) * g_ref[...]
    hb = (h * (1.0 + sc_ref[...]) + sh_ref[...]).astype(BF16)

    half = ROPE_DIM // 2
    ang = inv_ref[...] * pos_ref[...]
    cs_t = jnp.concatenate([jnp.cos(ang), jnp.sin(ang)], axis=0).T
    tab = _dot(jnp.concatenate(_split2(cs_t), axis=1), spread_ref[...])
    lane = lax.broadcasted_iota(jnp.int32, (1, LANES), 1) % HEAD_DIM
    cc = tab[:, 0:LANES] + jnp.where(lane < ROPE_DIM, 0.0, 1.0)
    s_lo = tab[:, LANES:2 * LANES]
    s_hi = tab[:, 2 * LANES:3 * LANES]

    def rope(t):
        return t * cc + pltpu.roll(t, LANES - half, 1) * s_lo + pltpu.roll(t, half, 1) * s_hi

    a_refs = (a0_ref, a1_ref, a2_ref)
    aw = len(ATTN_GROUPS) * GROUP_WIDTH
    gw = GROUP_WIDTH
    slab = 0
    for which in range(3):
        t = _dot(hb, w_ref[:, which * aw:(which + 1) * aw])
        for g, (_, dil) in enumerate(ATTN_GROUPS):
            halves = [t[:, g * gw + s * LANES:g * gw + (s + 1) * LANES] for s in range(gw // LANES)]
            if which < 2:
                halves = [rope(hv) for hv in halves]
            if which == 0:
                halves = [hv * (HEAD_DIM ** -0.5) for hv in halves]
            for s, hv in enumerate(halves):
                if dil == 1:
                    a_refs[g][:, which * gw + s * LANES:which * gw + (s + 1) * LANES] = hv.astype(BF16)
                    continue
                dil_ref[slab] = hv
                for r in range(dil):
                    col = r * 3 * gw + which * gw + s * LANES
                    a_refs[g][:, col:col + LANES] = dil_ref[slab, pl.ds(r, tm // dil, stride=dil), :].astype(BF16)
                slab += 1

    first = (i % tiles_per_seq) == 0
    row0 = lax.broadcasted_iota(jnp.int32, (tm, 1), 0) == 0
    shift_col = 3 * aw
    gate_col = shift_col + mu_ref.shape[1]
    col = 0
    for ref in (rr_ref, rk_ref, rv_ref, rl_ref):
        n = ref.shape[1]
        u = _dot(hb, w_ref[:, shift_col + col:shift_col + col + n])
        carry = jnp.where(first, 0.0, carry_ref[:, col:col + n])
        prev = jnp.where(row0, carry, pltpu.roll(u, 1, 0))
        carry_ref[:, col:col + n] = u[tm - 1:tm, :]
        ref[...] = (u + (prev - u) * mu_ref[:, col:col + n]).astype(ref.dtype)
        col += n

    for j in range(2):
        gj = _dot(hb, w_ref[:, gate_col + j * d:gate_col + (j + 1) * d])
        gate_ref[:, j * d:(j + 1) * d] = _sigmoid(gj).astype(BF16)


def _rope_spread():
    half = ROPE_DIM // 2
    cols = jnp.arange(3 * LANES)
    block, in_head = cols // LANES, cols % HEAD_DIM
    freq = in_head % half
    src_row = jnp.where(block == 0, freq, half + freq)
    sign = jnp.where(block == 1, -1.0, 1.0)
    used = jnp.where(block == 0, in_head < ROPE_DIM,
                     jnp.where(block == 1, in_head < half, (in_head >= half) & (in_head < ROPE_DIM)))
    one_piece = jnp.where((jnp.arange(2 * half)[:, None] == src_row[None, :]) & used[None, :], sign[None, :], 0.0)
    return jnp.concatenate([one_piece, one_piece], axis=0).astype(BF16)


def _inproj(x2, pos_rows, sc, sh, gain, inv_col, w, mu, *, seq, tm):
    t_rows, d = x2.shape
    tiles_per_seq = seq // tm
    aw = len(ATTN_GROUPS) * GROUP_WIDTH
    shift_w = mu.shape[1]
    assert w.shape[1] == 3 * aw + shift_w + 2 * d
    lora_w = shift_w - 3 * d
    row = lambda i: (i, 0)
    per_batch = lambda i: (i // tiles_per_seq, 0, 0)
    dils = [dil for _, dil in ATTN_GROUPS]
    out_shape = (
        [jax.ShapeDtypeStruct((t_rows // dil, dil * aw), BF16) for dil in dils]
        + [jax.ShapeDtypeStruct((t_rows, d), BF16)] * 3
        + [jax.ShapeDtypeStruct((t_rows, lora_w), F32), jax.ShapeDtypeStruct((t_rows, 2 * d), BF16)]
    )
    out_specs = (
        [pl.BlockSpec((tm // dil, dil * aw), row) for dil in dils]
        + [pl.BlockSpec((tm, d), row)] * 3
        + [pl.BlockSpec((tm, lora_w), row), pl.BlockSpec((tm, 2 * d), row)]
    )
    n_slabs = sum(3 * GROUP_WIDTH // LANES for dil in dils if dil > 1)
    spread = _rope_spread()
    return pl.pallas_call(
        functools.partial(_inproj_kernel, tiles_per_seq=tiles_per_seq),
        out_shape=out_shape,
        grid=(t_rows // tm,),
        in_specs=[
            pl.BlockSpec((tm, d), row),
            pl.BlockSpec((None, 1, tm), lambda i: (i, 0, 0)),
            pl.BlockSpec((None, 1, d), per_batch),
            pl.BlockSpec((None, 1, d), per_batch),
            _const_spec((1, d)),
            _const_spec(inv_col.shape),
            _const_spec(spread.shape),
            _const_spec(w.shape),
            _const_spec((1, shift_w)),
        ],
        out_specs=out_specs,
        scratch_shapes=[pltpu.VMEM((1, shift_w), F32), pltpu.VMEM((n_slabs, tm, LANES), F32)],
        compiler_params=pltpu.CompilerParams(
            dimension_semantics=("arbitrary",), vmem_limit_bytes=VMEM_LIMIT),
        name="inproj",
    )(x2, pos_rows, sc, sh, gain, inv_col, spread, w, mu)


def _attn_kernel(*refs, blocks_per_subseq):
    n_groups = len(blocks_per_subseq)
    in_refs, out_refs = refs[:2 * n_groups], refs[2 * n_groups:]
    m = pl.program_id(1)
    blk = ATTN_BLOCK
    gw = GROUP_WIDTH
    nh = ATTN_HEADS_PER_GROUP
    row = lax.broadcasted_iota(jnp.int32, (nh * blk, blk), 0) % blk
    colk = lax.broadcasted_iota(jnp.int32, (nh * blk, blk), 1)
    mask_cur = colk <= row
    mask_prev_band = colk >= row
    head_of_lane = lax.broadcasted_iota(jnp.int32, (1, gw), 1) // HEAD_DIM
    groups = range(n_groups)
    cur = [in_refs[2 * g] for g in groups]
    prev = [in_refs[2 * g + 1] for g in groups]
    q4, sc, sp = [], [], []
    for g in groups:
        q = cur[g][:, 0:gw]
        zq = jnp.zeros_like(q)
        q4.append(jnp.concatenate([jnp.where(head_of_lane == hd, q, zq) for hd in range(nh)], axis=0))
    for g in groups:
        mask_prev = mask_prev_band & ((m % blocks_per_subseq[g]) > 0)
        sc.append(jnp.where(mask_cur, _dot_nt(q4[g], cur[g][:, gw:2 * gw]), NEG_INF))
        sp.append(jnp.where(mask_prev, _dot_nt(q4[g], prev[g][:, gw:2 * gw]), NEG_INF))
    mx = [jnp.max(jnp.maximum(sc[g], sp[g]), axis=-1, keepdims=True) for g in groups]
    pc = [jnp.exp(sc[g] - mx[g]) for g in groups]
    pp = [jnp.exp(sp[g] - mx[g]) for g in groups]
    den = [jnp.sum(pc[g] + pp[g], axis=-1, keepdims=True) for g in groups]
    pv = [_dot(pc[g].astype(BF16), cur[g][:, 2 * gw:3 * gw]) + _dot(pp[g].astype(BF16), prev[g][:, 2 * gw:3 * gw])
          for g in groups]
    for g in groups:
        o4 = pv[g] * (1.0 / den[g])
        l4 = mx[g] + jnp.log(den[g])
        o_acc = o4[0:blk]
        l_acc = jnp.broadcast_to(l4[0:blk], (blk, gw))
        for hd in range(1, nh):
            hm = head_of_lane == hd
            o_acc = jnp.where(hm, o4[hd * blk:(hd + 1) * blk], o_acc)
            l_acc = jnp.where(hm, l4[hd * blk:(hd + 1) * blk], l_acc)
        out_refs[2 * g][...] = o_acc.astype(BF16)
        out_refs[2 * g + 1][...] = l_acc


def _attn(a_list, *, batch, seq):
    assert all(w // dil == ATTN_BLOCK for w, dil in ATTN_GROUPS)
    blk, gw = ATTN_BLOCK, GROUP_WIDTH
    n_blocks = seq // blk
    ins, in_specs, out_shape, out_specs, nbs = [], [], [], [], []
    for a, (_, dil) in zip(a_list, ATTN_GROUPS):
        assert seq % (dil * blk) == 0
        nb = n_blocks // dil
        nbs.append(nb)
        view = a.reshape(batch, seq // dil, dil * 3 * gw)
        cur = lambda b, m, nb=nb: (b, m % nb, m // nb)
        prev = lambda b, m, nb=nb: (b, jnp.maximum(m % nb - 1, 0), m // nb)
        ins += [view, view]
        in_specs += [pl.BlockSpec((None, blk, 3 * gw), cur), pl.BlockSpec((None, blk, 3 * gw), prev)]
        out_shape += [jax.ShapeDtypeStruct((batch, seq // dil, dil * gw), BF16),
                      jax.ShapeDtypeStruct((batch, seq // dil, dil * gw), F32)]
        out_specs += [pl.BlockSpec((None, blk, gw), cur)] * 2
    outs = pl.pallas_call(
        functools.partial(_attn_kernel, blocks_per_subseq=tuple(nbs)),
        out_shape=out_shape,
        grid=(batch, n_blocks),
        in_specs=in_specs,
        out_specs=out_specs,
        compiler_params=pltpu.CompilerParams(dimension_semantics=("arbitrary", "arbitrary")),
        name="attn",
    )(*ins)
    return [o.reshape(-1, o.shape[-1]) for o in outs]


def _seg_sum(z, seg_ones):
    rows, c = z.shape
    n = c // LANES
    zs = jnp.concatenate([z[:, j * LANES:(j + 1) * LANES] for j in range(n)], axis=0).astype(BF16)
    out = _dot(zs, seg_ones)
    return jnp.concatenate([out[j * rows:(j + 1) * rows] for j in range(n)], axis=1)


def _wkv_kernel(rr_ref, rk_ref, rv_ref, rl_ref, w0_ref, w2_ref, a0_ref, a2_ref, g2_ref, kk_ref, ka_ref,
                rkp_ref, gnw_ref, gnb_ref, o_ref,
                state_ref, at_ref, rt_ref, bt_ref, kt_ref, v_ref, gam_ref, bonus_ref, g_ref):
    tt, c = rr_ref.shape
    n_pairs = c // LANES
    ck = WKV_CHUNK
    n_chunks = tt // ck
    staged = (at_ref, rt_ref, bt_ref, kt_ref, v_ref, gam_ref, bonus_ref, g_ref)

    @pl.when(pl.program_id(1) == 0)
    def _():
        state_ref[...] = jnp.zeros_like(state_ref)
        for ref in staged:
            ref[...] = jnp.zeros_like(ref)

    at_n, rt_n, bt_n, kt_n, v_n, gam_n, bonus_p, g_p = [ref[...] for ref in staged]

    lane = lax.broadcasted_iota(jnp.int32, (LANES, LANES), 1)
    rowi = lax.broadcasted_iota(jnp.int32, (LANES, LANES), 0)
    seg_ones = jnp.where(lane // HEAD_DIM == rowi // HEAD_DIM, 1.0, 0.0).astype(BF16)

    def stage_lora():
        lora = rl_ref[...]
        dwa = lora[:, 0:LANES]
        dg = lora[:, LANES:2 * LANES]
        zw = w0_ref[...] + _dot(jnp.tanh(dwa).astype(BF16), w2_ref[...])
        za = a0_ref[...] + _dot(dwa.astype(BF16), a2_ref[...])
        return zw, za, _dot(_sigmoid(dg).astype(BF16), g2_ref[...])

    tri = (lax.broadcasted_iota(jnp.int32, (ck, ck), 1)
           <= lax.broadcasted_iota(jnp.int32, (ck, ck), 0)).astype(BF16)

    def stage_chunk(ci, zw, za):
        rows = slice(ci * ck, (ci + 1) * ck)
        r = rr_ref[rows, :].astype(F32)
        k = rk_ref[rows, :].astype(F32)
        v = rv_ref[rows, :]
        zw = zw[rows]
        w_log = -(jnp.maximum(-zw, 0.0) + jnp.log(1.0 + jnp.exp(-jnp.abs(zw)))) - 0.5
        lw = -jnp.exp(w_log)
        a = _sigmoid(za[rows])
        kk = k * kk_ref[...]
        kk = kk * lax.rsqrt(jnp.maximum(_seg_sum(kk * kk, seg_ones), 1e-24))
        kmod = k * (1.0 + (a - 1.0) * ka_ref[...])
        bonus = _seg_sum(r * kmod * rkp_ref[...], seg_ones) * v.astype(F32)
        hi, lo = _split2(lw)
        cum = _dot(tri, hi) + _dot(tri, lo)
        e_in = jnp.exp(cum)
        e_neg = jnp.exp(-cum)
        return (
            (-kk * jnp.exp(cum - lw)).astype(BF16),
            (r * e_in).astype(BF16),
            (kk * a * e_neg).astype(BF16),
            (kmod * e_neg).astype(BF16),
            v,
            e_in[ck - 8:ck],
            bonus,
        )

    t_idx = lax.broadcasted_iota(jnp.int32, (ck, LANES), 0)
    i_idx = lax.broadcasted_iota(jnp.int32, (ck, LANES), 1) % ck
    strict = i_idx < t_idx
    incl = i_idx <= t_idx
    lo_mask = lax.broadcasted_iota(jnp.int32, (1, LANES), 1) < HEAD_DIM
    eye_cat = jnp.where(i_idx == t_idx, 1.0, 0.0)
    bk_row_head = lax.broadcasted_iota(jnp.int32, (LANES, 2 * LANES), 0) // HEAD_DIM
    bk_col_head = (lax.broadcasted_iota(jnp.int32, (LANES, 2 * LANES), 1) // ck) % 2
    bk_mask = bk_row_head == bk_col_head

    def entries(t):
        return jnp.stack([t[ci * ck:(ci + 1) * ck, p * LANES:(p + 1) * LANES]
                          for ci in range(n_chunks) for p in range(n_pairs)])

    def stack_heads(t):
        z = jnp.zeros_like(t)
        return jnp.concatenate([jnp.where(lo_mask, t, z), jnp.where(lo_mask, z, t)], axis=1).astype(BF16)

    def bmm(a_, b_):
        return jnp.einsum('pmk,pkn->pmn', a_, b_, preferred_element_type=F32)

    at = entries(at_n)
    rt = entries(rt_n)
    bt = entries(bt_n)
    kt = entries(kt_n)
    bk_t = jnp.swapaxes(jnp.concatenate([bt, bt, kt, kt], axis=1), 1, 2)
    bk_t = jnp.where(bk_mask, bk_t, jnp.zeros_like(bk_t))
    vw = stack_heads(entries(v_n))
    gcol = jnp.swapaxes(
        jnp.stack([gam_n[ci * 8:(ci + 1) * 8, p * LANES:(p + 1) * LANES]
                   for ci in range(n_chunks) for p in range(n_pairs)]), 1, 2)[:, :, 7:8]

    gm = bmm(jnp.concatenate([at, rt], axis=1), bk_t)
    a_ab = jnp.where(strict, gm[:, :ck, :LANES], 0.0)
    a_ak = jnp.where(strict, gm[:, :ck, LANES:], 0.0).astype(BF16)
    a_rb = jnp.where(incl, gm[:, ck:, :LANES], 0.0).astype(BF16)
    a_rk = jnp.where(incl, gm[:, ck:, LANES:], 0.0).astype(BF16)
    zw_t, za_t, g_new = stage_lora()
    new_chunks = []
    stage_order = iter(range(n_chunks))

    def stage_next():
        ci = next(stage_order, None)
        if ci is not None:
            new_chunks.append(stage_chunk(ci, zw_t, za_t))

    stage_next()
    tm_ = eye_cat + a_ab
    pw = bmm(a_ab.astype(BF16), stack_heads(a_ab))
    for step in range(4):
        both = bmm(jnp.concatenate([tm_, pw], axis=1).astype(BF16), stack_heads(pw))
        tm_ = tm_ + both[:, :ck, :]
        pw = both[:, ck:, :]
        if step % 2 == 0:
            stage_next()
    tm_ = tm_ + bmm(tm_.astype(BF16), stack_heads(pw))
    t_cat = tm_.astype(BF16)
    by_v = bmm(jnp.concatenate([a_ak, a_rk, bk_t[:, :, LANES:]], axis=1), vw)
    ta = bmm(t_cat, stack_heads(at))
    for _ in range(n_chunks):
        stage_next()
    tx = bmm(t_cat, stack_heads(by_v[:, :ck, :]))
    ra = rt + bmm(a_rb, stack_heads(ta))
    yc = by_v[:, ck:2 * ck, :] + bmm(a_rb, stack_heads(tx))
    dk = by_v[:, 2 * ck:, :]
    new_staged = [jnp.concatenate(parts, axis=0) for parts in zip(*new_chunks)] + [g_new]
    tara = jnp.concatenate([ta, ra], axis=1).astype(BF16)
    b_t = bk_t[:, :, :LANES]

    s = state_ref[...]
    y_rows = []
    for ci in range(n_chunks):
        es = slice(ci * n_pairs, (ci + 1) * n_pairs)
        o1 = bmm(tara[es], s.astype(BF16))
        u = o1[:, :ck, :] + tx[es]
        y = o1[:, ck:, :] + yc[es]
        y_rows.append(jnp.concatenate([y[p] for p in range(n_pairs)], axis=1))
        s = (s + bmm(b_t[es], stack_heads(u)) + dk[es]) * gcol[es]
    state_ref[...] = s

    y = jnp.concatenate(y_rows, axis=0)
    inv_n = 1.0 / HEAD_DIM
    mean = _seg_sum(y, seg_ones) * inv_n
    ycen = y - mean
    var = _seg_sum(ycen * ycen, seg_ones) * inv_n
    yn = ycen * lax.rsqrt(var + GN_EPS) * gnw_ref[...] + gnb_ref[...]
    o_ref[...] = ((yn + bonus_p) * g_p).astype(o_ref.dtype)

    for ref, val in zip(staged, new_staged):
        ref[...] = val


def _wkv(rr, rk, rv, rl, w0, w2p, a0, a2p, g2, k_k, k_a, r_k, gn_w, gn_b, *, batch, seq, tt):
    t_rows, c = rr.shape
    steps = seq // tt
    row_in = lambda b, j: (b * steps + jnp.minimum(j, steps - 1), 0)
    row_out = lambda b, j: (b * steps + jnp.maximum(j - 1, 0), 0)
    vec = _const_spec((1, c))
    n_gam = (tt // WKV_CHUNK) * 8
    return pl.pallas_call(
        _wkv_kernel,
        out_shape=jax.ShapeDtypeStruct((t_rows, c), BF16),
        grid=(batch, steps + 1),
        in_specs=[
            pl.BlockSpec((tt, c), row_in), pl.BlockSpec((tt, c), row_in), pl.BlockSpec((tt, c), row_in),
            pl.BlockSpec((tt, rl.shape[1]), row_in),
            vec, _const_spec(w2p.shape), vec, _const_spec(a2p.shape), _const_spec(g2.shape),
            vec, vec, vec, vec, vec,
        ],
        out_specs=pl.BlockSpec((tt, c), row_out),
        scratch_shapes=[pltpu.VMEM((c // LANES, LANES, LANES), F32)]
        + [pltpu.VMEM((tt, c), BF16)] * 5
        + [pltpu.VMEM((n_gam, c), F32), pltpu.VMEM((tt, c), F32), pltpu.VMEM((tt, c), F32)],
        compiler_params=pltpu.CompilerParams(
            dimension_semantics=("arbitrary", "arbitrary"), vmem_limit_bytes=VMEM_LIMIT),
        name="wkv",
    )(rr, rk, rv, rl, w0, w2p, a0, a2p, g2, k_k, k_a, r_k, gn_w, gn_b)


def _branch_kernel(o0_ref, l0_ref, o1_ref, l1_ref, o2_ref, l2_ref, gate_ref, orw_ref, x_ref,
                   wba_ref, wbr_ref, wo_ref, gt_ref, sc_ref, sh_ref, gpost_ref, gpre_ref,
                   x1_ref, h2_ref, nat_ref):
    tm, d = x_ref.shape
    gw = GROUP_WIDTH

    def natural(ref, dil, slab):
        if dil == 1:
            return ref[...].astype(F32)
        for r in range(dil):
            for s in range(gw // LANES):
                nat_ref[slab + s, pl.ds(r, tm // dil, stride=dil), :] = (
                    ref[:, r * gw + s * LANES:r * gw + (s + 1) * LANES].astype(F32))
        return jnp.concatenate([nat_ref[slab + s] for s in range(gw // LANES)], axis=1)

    dils = [dil for _, dil in ATTN_GROUPS]
    per_group = gw // LANES
    o0, o1, o2 = [natural(ref, dil, 2 * g * per_group)
                  for g, (ref, dil) in enumerate(zip((o0_ref, o1_ref, o2_ref), dils))]
    l0, l1, l2 = [natural(ref, dil, (2 * g + 1) * per_group)
                  for g, (ref, dil) in enumerate(zip((l0_ref, l1_ref, l2_ref), dils))]
    mx = jnp.maximum(jnp.maximum(l0, l1), l2)
    e0, e1, e2 = jnp.exp(l0 - mx), jnp.exp(l1 - mx), jnp.exp(l2 - mx)
    o_attn = (e0 * o0 + e1 * o1 + e2 * o2) / (e0 + e1 + e2)
    ya = _dot(o_attn.astype(BF16), wba_ref[...].astype(BF16))
    yr = _dot(orw_ref[...], wbr_ref[...].astype(BF16))
    merged = gate_ref[:, 0:d].astype(F32) * ya + gate_ref[:, d:2 * d].astype(F32) * yr
    mix = _dot(merged.astype(BF16), wo_ref[...].astype(BF16))
    x1 = x_ref[...] + gt_ref[...] * (_rms(mix) * gpost_ref[...])
    x1_ref[...] = x1
    h2 = _rms(x1) * gpre_ref[...]
    h2_ref[...] = (h2 * (1.0 + sc_ref[...]) + sh_ref[...]).astype(BF16)


def _branch(attn_outs, gates, o_rwkv, x2, wba, wbr, wo, gt, sc, sh, gpost, gpre, *, seq, tm):
    t_rows, d = x2.shape
    tiles_per_seq = seq // tm
    row = lambda i: (i, 0)
    per_batch = lambda i: (i // tiles_per_seq, 0, 0)
    gw = GROUP_WIDTH
    attn_specs = [pl.BlockSpec((tm // dil, dil * gw), row) for _, dil in ATTN_GROUPS for _ in range(2)]
    n_slabs = 2 * len(ATTN_GROUPS) * (gw // LANES)
    return pl.pallas_call(
        _branch_kernel,
        out_shape=[jax.ShapeDtypeStruct((t_rows, d), F32), jax.ShapeDtypeStruct((t_rows, d), BF16)],
        grid=(t_rows // tm,),
        scratch_shapes=[pltpu.VMEM((n_slabs, tm, LANES), F32)],
        in_specs=attn_specs + [
            pl.BlockSpec((tm, 2 * d), row), pl.BlockSpec((tm, d), row), pl.BlockSpec((tm, d), row),
            _const_spec(wba.shape), _const_spec(wbr.shape), _const_spec(wo.shape),
            pl.BlockSpec((None, 1, d), per_batch), pl.BlockSpec((None, 1, d), per_batch),
            pl.BlockSpec((None, 1, d), per_batch),
            _const_spec((1, d)), _const_spec((1, d)),
        ],
        out_specs=[pl.BlockSpec((tm, d), row), pl.BlockSpec((tm, d), row)],
        compiler_params=pltpu.CompilerParams(
            dimension_semantics=("arbitrary",), vmem_limit_bytes=VMEM_LIMIT),
        name="branch",
    )(*attn_outs, gates, o_rwkv, x2, wba, wbr, wo, gt, sc, sh, gpost, gpre)


def _ffn_kernel(h_ref, x_ref, w1_ref, w2_ref, gt_ref, gpost_ref, o_ref, *, tf):
    h = h_ref[...]
    acc = jnp.zeros(x_ref.shape, F32)
    for j in range(w1_ref.shape[1] // tf):
        a = jnp.maximum(_dot(h, w1_ref[:, j * tf:(j + 1) * tf].astype(BF16)), 0.0)
        acc = acc + _dot((a * a).astype(BF16), w2_ref[j * tf:(j + 1) * tf, :].astype(BF16))
    o_ref[...] = x_ref[...] + gt_ref[...] * (_rms(acc) * gpost_ref[...])


def _ffn(h2, x1, w1, w2, gt, gpost, *, seq, tm, tf):
    t_rows, d = x1.shape
    tiles_per_seq = seq // tm
    row = lambda i: (i, 0)
    per_batch = lambda i: (i // tiles_per_seq, 0, 0)
    return pl.pallas_call(
        functools.partial(_ffn_kernel, tf=tf),
        out_shape=jax.ShapeDtypeStruct((t_rows, d), F32),
        grid=(t_rows // tm,),
        in_specs=[
            pl.BlockSpec((tm, d), row), pl.BlockSpec((tm, d), row),
            _const_spec(w1.shape), _const_spec(w2.shape),
            pl.BlockSpec((None, 1, d), per_batch), _const_spec((1, d)),
        ],
        out_specs=pl.BlockSpec((tm, d), row),
        compiler_params=pltpu.CompilerParams(
            dimension_semantics=("arbitrary",), vmem_limit_bytes=VMEM_LIMIT),
        name="ffn",
    )(h2, x1, w1, w2, gt, gpost)


def kernel(x, c, positions, ada_w, ada_b, norm_mix_pre, norm_mix_post, norm_ffn_pre, norm_ffn_post, w_in, shift_mu, decay_w0, decay_w2, iclr_a0, iclr_a2, gate_g2, k_k, k_a, r_k, gn_w, gn_b, w_branch, w_out, w_ff1, w_ff2):
    batch, seq, d = x.shape
    depth = ada_w.shape[0]
    t_rows = batch * seq
    n_groups = len(ATTN_GROUPS)
    attn_out_w = GROUP_WIDTH
    shift_w = shift_mu.shape[1]
    assert shift_w == 3 * d + DECAY_LORA + ICLR_LORA + GATE_LORA
    assert DECAY_LORA + ICLR_LORA == LANES and GATE_LORA == LANES
    tm = 512

    x2 = x.reshape(t_rows, d)
    pos_rows = positions.astype(F32).reshape(t_rows // tm, 1, tm)
    half = ROPE_DIM // 2
    inv_col = (ROPE_THETA ** (-jnp.arange(half, dtype=F32) * 2.0 / ROPE_DIM)).reshape(half, 1)
    c_pad = jnp.pad(c, ((0, 8 - batch), (0, 0)))
    vec = lambda p: p.reshape(1, -1)

    for l in range(depth):
        mod = _mod(c_pad, ada_w[l], ada_b[l].reshape(1, -1))[:batch]
        sh1, sc1, gt1, sh2, sc2, gt2 = [m.reshape(batch, 1, d) for m in jnp.split(mod, 6, axis=-1)]

        a0, a1, a2, rr, rk, rv, rl, gates = _inproj(
            x2, pos_rows, sc1, sh1, vec(norm_mix_pre[l]), inv_col, w_in[l].astype(BF16), vec(shift_mu[l]),
            seq=seq, tm=tm)

        attn_outs = _attn([a0, a1, a2], batch=batch, seq=seq)

        zeros = jnp.zeros((DECAY_LORA, d), BF16)
        w2p = jnp.concatenate([decay_w2[l].astype(BF16), zeros], axis=0)
        a2p = jnp.concatenate([zeros, iclr_a2[l].astype(BF16)], axis=0)
        o_rwkv = _wkv(rr, rk, rv, rl, vec(decay_w0[l]), w2p, vec(iclr_a0[l]), a2p,
                      gate_g2[l].astype(BF16), vec(k_k[l]), vec(k_a[l]), vec(r_k[l]),
                      vec(gn_w[l]), vec(gn_b[l]), batch=batch, seq=seq, tt=256)

        wb = w_branch[l]
        x1, h2 = _branch(attn_outs, gates, o_rwkv, x2, wb[:attn_out_w], wb[attn_out_w:],
                         w_out[l], gt1, sc2, sh2, vec(norm_mix_post[l]),
                         vec(norm_ffn_pre[l]), seq=seq, tm=tm)
        x2 = _ffn(h2, x1, w_ff1[l], w_ff2[l], gt2, vec(norm_ffn_post[l]),
                  seq=seq, tm=tm, tf=512)
    return x2.reshape(batch, seq, d)
```

```python
import functools

import jax
import jax.numpy as jnp
from jax import lax
from jax.experimental import pallas as pl
from jax.experimental.pallas import tpu as pltpu

F32 = jnp.float32
BF16 = jnp.bfloat16

HEAD_DIM = 64
ATTN_GROUPS = ((128, 1), (512, 4), (2048, 16))
ATTN_HEADS_PER_GROUP = 4
ATTN_BLOCK = 128
GROUP_WIDTH = ATTN_HEADS_PER_GROUP * HEAD_DIM
ROPE_THETA = 500000.0
ROPE_DIM = HEAD_DIM // 4
DECAY_LORA = 64
ICLR_LORA = 64
GATE_LORA = 128
NORM_EPS = 1e-6
GN_EPS = 64e-5
NEG_INF = -1e30

LANES = 128
WKV_CHUNK = 64
VMEM_LIMIT = 56 * 1024 * 1024


def _dot(a, b):
    return jnp.dot(a, b, preferred_element_type=F32)


def _dot_nt(a, b):
    return lax.dot_general(a, b, (((1,), (1,)), ((), ())), preferred_element_type=F32)


def _split2(x):
    hi = x.astype(BF16)
    lo = (x - hi.astype(F32)).astype(BF16)
    return hi, lo


def _sigmoid(x):
    return 1.0 / (1.0 + jnp.exp(-x))


def _rms(t):
    return t * lax.rsqrt(jnp.mean(t * t, axis=-1, keepdims=True) + NORM_EPS)


def _const_spec(shape):
    return pl.BlockSpec(shape, lambda *_: (0,) * len(shape), pipeline_mode=pl.Buffered(1))


def _mod_kernel(c_ref, w_ref, b_ref, o_ref):
    c = c_ref[...]
    a = c * _sigmoid(c)
    ah, al = _split2(a)
    wh, wl = _split2(w_ref[...])
    o_ref[...] = _dot(ah, wh) + _dot(ah, wl) + _dot(al, wh) + b_ref[...]


def _mod(c_pad, ada_w, ada_b):
    rows, d = c_pad.shape
    n = ada_w.shape[1]
    tn = 768
    return pl.pallas_call(
        _mod_kernel,
        out_shape=jax.ShapeDtypeStruct((rows, n), F32),
        grid=(n // tn,),
        in_specs=[
            pl.BlockSpec((rows, d), lambda j: (0, 0)),
            pl.BlockSpec((d, tn), lambda j: (0, j)),
            pl.BlockSpec((1, tn), lambda j: (0, j)),
        ],
        out_specs=pl.BlockSpec((rows, tn), lambda j: (0, j)),
        name="mod",
    )(c_pad, ada_w, ada_b)


def _inproj_kernel(x_ref, pos_ref, sc_ref, sh_ref, g_ref, inv_ref, spread_ref, w_ref, mu_ref,
                   a0_ref, a1_ref, a2_ref, rr_ref, rk_ref, rv_ref, rl_ref, gate_ref, carry_ref, dil_ref,
                   *, tiles_per_seq):
    i = pl.program_id(0)
    tm = x_ref.shape[0]
    d = x_ref.shape[1]
    h = _rms(x_ref[...]) * g_ref[...]
    hb = (h * (1.0 + sc_ref[...]) + sh_ref[...]).astype(BF16)

    half = ROPE_DIM // 2
    ang = inv_ref[...] * pos_ref[...]
    cs_t = jnp.concatenate([jnp.cos(ang), jnp.sin(ang)], axis=0).T
    tab = _dot(jnp.concatenate(_split2(cs_t), axis=1), spread_ref[...])
    lane = lax.broadcasted_iota(jnp.int32, (1, LANES), 1) % HEAD_DIM
    cc = tab[:, 0:LANES] + jnp.where(lane < ROPE_DIM, 0.0, 1.0)
    s_lo = tab[:, LANES:2 * LANES]
    s_hi = tab[:, 2 * LANES:3 * LANES]

    def rope(t):
        return t * cc + pltpu.roll(t, LANES - half, 1) * s_lo + pltpu.roll(t, half, 1) * s_hi

    a_refs = (a0_ref, a1_ref, a2_ref)
    aw = len(ATTN_GROUPS) * GROUP_WIDTH
    gw = GROUP_WIDTH
    slab = 0
    for which in range(3):
        t = _dot(hb, w_ref[:, which * aw:(which + 1) * aw])
        for g, (_, dil) in enumerate(ATTN_GROUPS):
            halves = [t[:, g * gw + s * LANES:g * gw + (s + 1) * LANES] for s in range(gw // LANES)]
            if which < 2:
                halves = [rope(hv) for hv in halves]
            if which == 0:
                halves = [hv * (HEAD_DIM ** -0.5) for hv in halves]
            for s, hv in enumerate(halves):
                if dil == 1:
                    a_refs[g][:, which * gw + s * LANES:which * gw + (s + 1) * LANES] = hv.astype(BF16)
                    continue
                dil_ref[slab] = hv
                for r in range(dil):
                    col = r * 3 * gw + which * gw + s * LANES
                    a_refs[g][:, col:col + LANES] = dil_ref[slab, pl.ds(r, tm // dil, stride=dil), :].astype(BF16)
                slab += 1

    first = (i % tiles_per_seq) == 0
    row0 = lax.broadcasted_iota(jnp.int32, (tm, 1), 0) == 0
    shift_col = 3 * aw
    gate_col = shift_col + mu_ref.shape[1]
    col = 0
    for ref in (rr_ref, rk_ref, rv_ref, rl_ref):
        n = ref.shape[1]
        u = _dot(hb, w_ref[:, shift_col + col:shift_col + col + n])
        carry = jnp.where(first, 0.0, carry_ref[:, col:col + n])
        prev = jnp.where(row0, carry, pltpu.roll(u, 1, 0))
        carry_ref[:, col:col + n] = u[tm - 1:tm, :]
        ref[...] = (u + (prev - u) * mu_ref[:, col:col + n]).astype(ref.dtype)
        col += n

    for j in range(2):
        gj = _dot(hb, w_ref[:, gate_col + j * d:gate_col + (j + 1) * d])
        gate_ref[:, j * d:(j + 1) * d] = _sigmoid(gj).astype(BF16)


def _rope_spread():
    half = ROPE_DIM // 2
    cols = jnp.arange(3 * LANES)
    block, in_head = cols // LANES, cols % HEAD_DIM
    freq = in_head % half
    src_row = jnp.where(block == 0, freq, half + freq)
    sign = jnp.where(block == 1, -1.0, 1.0)
    used = jnp.where(block == 0, in_head < ROPE_DIM,
                     jnp.where(block == 1, in_head < half, (in_head >= half) & (in_head < ROPE_DIM)))
    one_piece = jnp.where((jnp.arange(2 * half)[:, None] == src_row[None, :]) & used[None, :], sign[None, :], 0.0)
    return jnp.concatenate([one_piece, one_piece], axis=0).astype(BF16)


def _inproj(x2, pos_rows, sc, sh, gain, inv_col, w, mu, *, seq, tm):
    t_rows, d = x2.shape
    tiles_per_seq = seq // tm
    aw = len(ATTN_GROUPS) * GROUP_WIDTH
    shift_w = mu.shape[1]
    assert w.shape[1] == 3 * aw + shift_w + 2 * d
    lora_w = shift_w - 3 * d
    row = lambda i: (i, 0)
    per_batch = lambda i: (i // tiles_per_seq, 0, 0)
    dils = [dil for _, dil in ATTN_GROUPS]
    out_shape = (
        [jax.ShapeDtypeStruct((t_rows // dil, dil * aw), BF16) for dil in dils]
        + [jax.ShapeDtypeStruct((t_rows, d), BF16)] * 3
        + [jax.ShapeDtypeStruct((t_rows, lora_w), F32), jax.ShapeDtypeStruct((t_rows, 2 * d), BF16)]
    )
    out_specs = (
        [pl.BlockSpec((tm // dil, dil * aw), row) for dil in dils]
        + [pl.BlockSpec((tm, d), row)] * 3
        + [pl.BlockSpec((tm, lora_w), row), pl.BlockSpec((tm, 2 * d), row)]
    )
    n_slabs = sum(3 * GROUP_WIDTH // LANES for dil in dils if dil > 1)
    spread = _rope_spread()
    return pl.pallas_call(
        functools.partial(_inproj_kernel, tiles_per_seq=tiles_per_seq),
        out_shape=out_shape,
        grid=(t_rows // tm,),
        in_specs=[
            pl.BlockSpec((tm, d), row),
            pl.BlockSpec((None, 1, tm), lambda i: (i, 0, 0)),
            pl.BlockSpec((None, 1, d), per_batch),
            pl.BlockSpec((None, 1, d), per_batch),
            _const_spec((1, d)),
            _const_spec(inv_col.shape),
            _const_spec(spread.shape),
            _const_spec(w.shape),
            _const_spec((1, shift_w)),
        ],
        out_specs=out_specs,
        scratch_shapes=[pltpu.VMEM((1, shift_w), F32), pltpu.VMEM((n_slabs, tm, LANES), F32)],
        compiler_params=pltpu.CompilerParams(
            dimension_semantics=("arbitrary",), vmem_limit_bytes=VMEM_LIMIT),
        name="inproj",
    )(x2, pos_rows, sc, sh, gain, inv_col, spread, w, mu)


def _attn_kernel(*refs, pairs_per_subseq):
    n_groups = len(pairs_per_subseq)
    in_refs, out_refs = refs[:2 * n_groups], refs[2 * n_groups:]
    m = pl.program_id(1)
    blk = ATTN_BLOCK
    gw = GROUP_WIDTH
    nh = ATTN_HEADS_PER_GROUP
    row = lax.broadcasted_iota(jnp.int32, (nh * blk, blk), 0) % blk
    colk = lax.broadcasted_iota(jnp.int32, (nh * blk, blk), 1)
    mask_cur = colk <= row
    mask_prev_band = colk >= row
    head_of_lane = lax.broadcasted_iota(jnp.int32, (1, gw), 1) // HEAD_DIM

    probs = []
    for g in range(n_groups):
        cur, prev = in_refs[2 * g], in_refs[2 * g + 1]
        first = mask_prev_band & ((m % pairs_per_subseq[g]) > 0)
        probs.append((g, cur, slice(0, blk), prev, slice(0, blk), first))
        probs.append((g, cur, slice(blk, 2 * blk), cur, slice(0, blk), mask_prev_band))
    idx = range(len(probs))
    q4 = []
    for _, cur, rows, _, _, _ in probs:
        q = cur[rows, 0:gw]
        zq = jnp.zeros_like(q)
        q4.append(jnp.concatenate([jnp.where(head_of_lane == hd, q, zq) for hd in range(nh)], axis=0))
    sc = [jnp.where(mask_cur, _dot_nt(q4[i], p[1][p[2], gw:2 * gw]), NEG_INF) for i, p in zip(idx, probs)]
    sp = [jnp.where(p[5], _dot_nt(q4[i], p[3][p[4], gw:2 * gw]), NEG_INF) for i, p in zip(idx, probs)]
    mx = [jnp.max(jnp.maximum(sc[i], sp[i]), axis=-1, keepdims=True) for i in idx]
    pc = [jnp.exp(sc[i] - mx[i]) for i in idx]
    pp = [jnp.exp(sp[i] - mx[i]) for i in idx]
    den = [jnp.sum(pc[i] + pp[i], axis=-1, keepdims=True) for i in idx]
    pcb = [pc[i].astype(BF16) for i in idx]
    ppb = [pp[i].astype(BF16) for i in idx]
    low_half = lax.broadcasted_iota(jnp.int32, (1, LANES), 1) < HEAD_DIM
    pv = []
    for i, p in zip(idx, probs):
        per_pair = []
        for j in range(nh // 2):
            rows2 = slice(2 * j * blk, (2 * j + 2) * blk)
            vcol = slice(2 * gw + j * LANES, 2 * gw + (j + 1) * LANES)
            per_pair.append(_dot(pcb[i][rows2], p[1][p[2], vcol]) + _dot(ppb[i][rows2], p[3][p[4], vcol]))
        pv.append(per_pair)
    for i, (g, _, rows, _, _, _) in zip(idx, probs):
        inv_den = 1.0 / den[i]
        l4 = mx[i] + jnp.log(den[i])
        o_pairs, l_pairs = [], []
        for j in range(nh // 2):
            lo_rows = slice(2 * j * blk, (2 * j + 1) * blk)
            hi_rows = slice((2 * j + 1) * blk, (2 * j + 2) * blk)
            scaled = pv[i][j] * inv_den[2 * j * blk:(2 * j + 2) * blk]
            o_pairs.append(jnp.where(low_half, scaled[0:blk], scaled[blk:2 * blk]))
            l_pairs.append(jnp.where(low_half, l4[lo_rows], l4[hi_rows]))
        out_refs[2 * g][rows, :] = jnp.concatenate(o_pairs, axis=1).astype(BF16)
        out_refs[2 * g + 1][rows, :] = jnp.concatenate(l_pairs, axis=1)


def _attn(a_list, *, batch, seq):
    assert all(w // dil == ATTN_BLOCK for w, dil in ATTN_GROUPS)
    blk, gw = ATTN_BLOCK, GROUP_WIDTH
    n_pairs = seq // (2 * blk)
    ins, in_specs, out_shape, out_specs, nps = [], [], [], [], []
    for a, (_, dil) in zip(a_list, ATTN_GROUPS):
        assert seq % (dil * 2 * blk) == 0
        np_ = n_pairs // dil
        nps.append(np_)
        view = a.reshape(batch, seq // dil, dil * 3 * gw)
        cur = lambda b, m, np_=np_: (b, m % np_, m // np_)
        prev = lambda b, m, np_=np_: (b, jnp.maximum(2 * (m % np_) - 1, 0), m // np_)
        ins += [view, view]
        in_specs += [pl.BlockSpec((None, 2 * blk, 3 * gw), cur), pl.BlockSpec((None, blk, 3 * gw), prev)]
        out_shape += [jax.ShapeDtypeStruct((batch, seq // dil, dil * gw), BF16),
                      jax.ShapeDtypeStruct((batch, seq // dil, dil * gw), F32)]
        out_specs += [pl.BlockSpec((None, 2 * blk, gw), cur)] * 2
    outs = pl.pallas_call(
        functools.partial(_attn_kernel, pairs_per_subseq=tuple(nps)),
        out_shape=out_shape,
        grid=(batch, n_pairs),
        in_specs=in_specs,
        out_specs=out_specs,
        compiler_params=pltpu.CompilerParams(dimension_semantics=("arbitrary", "arbitrary")),
        name="attn",
    )(*ins)
    return [o.reshape(-1, o.shape[-1]) for o in outs]


def _seg_sum(z, seg_ones):
    rows, c = z.shape
    n = c // LANES
    zs = jnp.concatenate([z[:, j * LANES:(j + 1) * LANES] for j in range(n)], axis=0).astype(BF16)
    out = _dot(zs, seg_ones)
    return jnp.concatenate([out[j * rows:(j + 1) * rows] for j in range(n)], axis=1)


def _wkv_kernel(rr_ref, rk_ref, rv_ref, rl_ref, w0_ref, w2_ref, a0_ref, a2_ref, g2_ref, kk_ref, ka_ref,
                rkp_ref, gnw_ref, gnb_ref, o_ref,
                state_ref, at_ref, rt_ref, bt_ref, kt_ref, v_ref, gam_ref, bonus_ref, g_ref):
    tt, c = rr_ref.shape
    n_pairs = c // LANES
    ck = WKV_CHUNK
    n_chunks = tt // ck
    staged = (at_ref, rt_ref, bt_ref, kt_ref, v_ref, gam_ref, bonus_ref, g_ref)

    @pl.when(pl.program_id(1) == 0)
    def _():
        state_ref[...] = jnp.zeros_like(state_ref)
        for ref in staged:
            ref[...] = jnp.zeros_like(ref)

    at_n, rt_n, bt_n, kt_n, v_n, gam_n, bonus_p, g_p = [ref[...] for ref in staged]

    lane = lax.broadcasted_iota(jnp.int32, (LANES, LANES), 1)
    rowi = lax.broadcasted_iota(jnp.int32, (LANES, LANES), 0)
    seg_ones = jnp.where(lane // HEAD_DIM == rowi // HEAD_DIM, 1.0, 0.0).astype(BF16)

    def stage_lora():
        lora = rl_ref[...]
        dwa = lora[:, 0:LANES]
        dg = lora[:, LANES:2 * LANES]
        zw = w0_ref[...] + _dot(jnp.tanh(dwa).astype(BF16), w2_ref[...])
        za = a0_ref[...] + _dot(dwa.astype(BF16), a2_ref[...])
        return zw, za, _dot(_sigmoid(dg).astype(BF16), g2_ref[...])

    tri = (lax.broadcasted_iota(jnp.int32, (ck, ck), 1)
           <= lax.broadcasted_iota(jnp.int32, (ck, ck), 0)).astype(BF16)

    def stage_chunk(ci, zw, za):
        rows = slice(ci * ck, (ci + 1) * ck)
        r = rr_ref[rows, :].astype(F32)
        k = rk_ref[rows, :].astype(F32)
        v = rv_ref[rows, :]
        zw = zw[rows]
        w_log = -(jnp.maximum(-zw, 0.0) + jnp.log(1.0 + jnp.exp(-jnp.abs(zw)))) - 0.5
        lw = -jnp.exp(w_log)
        a = _sigmoid(za[rows])
        kk = k * kk_ref[...]
        kk = kk * lax.rsqrt(jnp.maximum(_seg_sum(kk * kk, seg_ones), 1e-24))
        kmod = k * (1.0 + (a - 1.0) * ka_ref[...])
        bonus = _seg_sum(r * kmod * rkp_ref[...], seg_ones) * v.astype(F32)
        hi, lo = _split2(lw)
        cum = _dot(tri, hi) + _dot(tri, lo)
        e_in = jnp.exp(cum)
        e_neg = jnp.exp(-cum)
        return (
            (-kk * jnp.exp(cum - lw)).astype(BF16),
            (r * e_in).astype(BF16),
            (kk * a * e_neg).astype(BF16),
            (kmod * e_neg).astype(BF16),
            v,
            e_in[ck - 8:ck],
            bonus,
        )

    t_idx = lax.broadcasted_iota(jnp.int32, (ck, LANES), 0)
    i_idx = lax.broadcasted_iota(jnp.int32, (ck, LANES), 1) % ck
    strict = i_idx < t_idx
    incl = i_idx <= t_idx
    lo_mask = lax.broadcasted_iota(jnp.int32, (1, LANES), 1) < HEAD_DIM
    eye_cat = jnp.where(i_idx == t_idx, 1.0, 0.0)
    bk_row_head = lax.broadcasted_iota(jnp.int32, (LANES, 2 * LANES), 0) // HEAD_DIM
    bk_col_head = (lax.broadcasted_iota(jnp.int32, (LANES, 2 * LANES), 1) // ck) % 2
    bk_mask = bk_row_head == bk_col_head

    def entries(t):
        return jnp.stack([t[ci * ck:(ci + 1) * ck, p * LANES:(p + 1) * LANES]
                          for ci in range(n_chunks) for p in range(n_pairs)])

    def stack_heads(t):
        z = jnp.zeros_like(t)
        return jnp.concatenate([jnp.where(lo_mask, t, z), jnp.where(lo_mask, z, t)], axis=1).astype(BF16)

    def bmm(a_, b_):
        return jnp.einsum('pmk,pkn->pmn', a_, b_, preferred_element_type=F32)

    at = entries(at_n)
    rt = entries(rt_n)
    bt = entries(bt_n)
    kt = entries(kt_n)
    bk_t = jnp.swapaxes(jnp.concatenate([bt, bt, kt, kt], axis=1), 1, 2)
    bk_t = jnp.where(bk_mask, bk_t, jnp.zeros_like(bk_t))
    vw = stack_heads(entries(v_n))
    gcol = jnp.swapaxes(
        jnp.stack([gam_n[ci * 8:(ci + 1) * 8, p * LANES:(p + 1) * LANES]
                   for ci in range(n_chunks) for p in range(n_pairs)]), 1, 2)[:, :, 7:8]

    gm = bmm(jnp.concatenate([at, rt], axis=1), bk_t)
    a_ab = jnp.where(strict, gm[:, :ck, :LANES], 0.0)
    a_ak = jnp.where(strict, gm[:, :ck, LANES:], 0.0).astype(BF16)
    a_rb = jnp.where(incl, gm[:, ck:, :LANES], 0.0).astype(BF16)
    a_rk = jnp.where(incl, gm[:, ck:, LANES:], 0.0).astype(BF16)
    zw_t, za_t, g_new = stage_lora()
    new_chunks = []
    stage_order = iter(range(n_chunks))

    def stage_next():
        ci = next(stage_order, None)
        if ci is not None:
            new_chunks.append(stage_chunk(ci, zw_t, za_t))

    stage_next()
    tm_ = eye_cat + a_ab
    pw = bmm(a_ab.astype(BF16), stack_heads(a_ab))
    for step in range(4):
        both = bmm(jnp.concatenate([tm_, pw], axis=1).astype(BF16), stack_heads(pw))
        tm_ = tm_ + both[:, :ck, :]
        pw = both[:, ck:, :]
        if step % 2 == 0:
            stage_next()
    tm_ = tm_ + bmm(tm_.astype(BF16), stack_heads(pw))
    t_cat = tm_.astype(BF16)
    by_v = bmm(jnp.concatenate([a_ak, a_rk, bk_t[:, :, LANES:]], axis=1), vw)
    ta = bmm(t_cat, stack_heads(at))
    for _ in range(n_chunks):
        stage_next()
    tx = bmm(t_cat, stack_heads(by_v[:, :ck, :]))
    ra = rt + bmm(a_rb, stack_heads(ta))
    yc = by_v[:, ck:2 * ck, :] + bmm(a_rb, stack_heads(tx))
    dk = by_v[:, 2 * ck:, :]
    new_staged = [jnp.concatenate(parts, axis=0) for parts in zip(*new_chunks)] + [g_new]
    tara = jnp.concatenate([ta, ra], axis=1).astype(BF16)
    b_t = bk_t[:, :, :LANES]

    s = state_ref[...]
    y_rows = []
    for ci in range(n_chunks):
        es = slice(ci * n_pairs, (ci + 1) * n_pairs)
        o1 = bmm(tara[es], s.astype(BF16))
        u = o1[:, :ck, :] + tx[es]
        y = o1[:, ck:, :] + yc[es]
        y_rows.append(jnp.concatenate([y[p] for p in range(n_pairs)], axis=1))
        s = (s + bmm(b_t[es], stack_heads(u)) + dk[es]) * gcol[es]
    state_ref[...] = s

    y = jnp.concatenate(y_rows, axis=0)
    inv_n = 1.0 / HEAD_DIM
    mean = _seg_sum(y, seg_ones) * inv_n
    ycen = y - mean
    var = _seg_sum(ycen * ycen, seg_ones) * inv_n
    yn = ycen * lax.rsqrt(var + GN_EPS) * gnw_ref[...] + gnb_ref[...]
    o_ref[...] = ((yn + bonus_p) * g_p).astype(o_ref.dtype)

    for ref, val in zip(staged, new_staged):
        ref[...] = val


def _wkv(rr, rk, rv, rl, w0, w2p, a0, a2p, g2, k_k, k_a, r_k, gn_w, gn_b, *, batch, seq, tt):
    t_rows, c = rr.shape
    steps = seq // tt
    row_in = lambda b, j: (b * steps + jnp.minimum(j, steps - 1), 0)
    row_out = lambda b, j: (b * steps + jnp.maximum(j - 1, 0), 0)
    vec = _const_spec((1, c))
    n_gam = (tt // WKV_CHUNK) * 8
    return pl.pallas_call(
        _wkv_kernel,
        out_shape=jax.ShapeDtypeStruct((t_rows, c), BF16),
        grid=(batch, steps + 1),
        in_specs=[
            pl.BlockSpec((tt, c), row_in), pl.BlockSpec((tt, c), row_in), pl.BlockSpec((tt, c), row_in),
            pl.BlockSpec((tt, rl.shape[1]), row_in),
            vec, _const_spec(w2p.shape), vec, _const_spec(a2p.shape), _const_spec(g2.shape),
            vec, vec, vec, vec, vec,
        ],
        out_specs=pl.BlockSpec((tt, c), row_out),
        scratch_shapes=[pltpu.VMEM((c // LANES, LANES, LANES), F32)]
        + [pltpu.VMEM((tt, c), BF16)] * 5
        + [pltpu.VMEM((n_gam, c), F32), pltpu.VMEM((tt, c), F32), pltpu.VMEM((tt, c), F32)],
        compiler_params=pltpu.CompilerParams(
            dimension_semantics=("arbitrary", "arbitrary"), vmem_limit_bytes=VMEM_LIMIT),
        name="wkv",
    )(rr, rk, rv, rl, w0, w2p, a0, a2p, g2, k_k, k_a, r_k, gn_w, gn_b)


def _branch_kernel(o0_ref, l0_ref, o1_ref, l1_ref, o2_ref, l2_ref, gate_ref, orw_ref, x_ref,
                   wba_ref, wbr_ref, wo_ref, gt_ref, sc_ref, sh_ref, gpost_ref, gpre_ref,
                   x1_ref, h2_ref, nat_ref):
    tm, d = x_ref.shape
    gw = GROUP_WIDTH

    def natural(ref, dil, slab):
        if dil == 1:
            return ref[...].astype(F32)
        for r in range(dil):
            for s in range(gw // LANES):
                nat_ref[slab + s, pl.ds(r, tm // dil, stride=dil), :] = (
                    ref[:, r * gw + s * LANES:r * gw + (s + 1) * LANES].astype(F32))
        return jnp.concatenate([nat_ref[slab + s] for s in range(gw // LANES)], axis=1)

    dils = [dil for _, dil in ATTN_GROUPS]
    per_group = gw // LANES
    o0, o1, o2 = [natural(ref, dil, 2 * g * per_group)
                  for g, (ref, dil) in enumerate(zip((o0_ref, o1_ref, o2_ref), dils))]
    l0, l1, l2 = [natural(ref, dil, (2 * g + 1) * per_group)
                  for g, (ref, dil) in enumerate(zip((l0_ref, l1_ref, l2_ref), dils))]
    mx = jnp.maximum(jnp.maximum(l0, l1), l2)
    e0, e1, e2 = jnp.exp(l0 - mx), jnp.exp(l1 - mx), jnp.exp(l2 - mx)
    o_attn = (e0 * o0 + e1 * o1 + e2 * o2) / (e0 + e1 + e2)
    ya = _dot(o_attn.astype(BF16), wba_ref[...].astype(BF16))
    yr = _dot(orw_ref[...], wbr_ref[...].astype(BF16))
    merged = gate_ref[:, 0:d].astype(F32) * ya + gate_ref[:, d:2 * d].astype(F32) * yr
    mix = _dot(merged.astype(BF16), wo_ref[...].astype(BF16))
    x1 = x_ref[...] + gt_ref[...] * (_rms(mix) * gpost_ref[...])
    x1_ref[...] = x1
    h2 = _rms(x1) * gpre_ref[...]
    h2_ref[...] = (h2 * (1.0 + sc_ref[...]) + sh_ref[...]).astype(BF16)


def _branch(attn_outs, gates, o_rwkv, x2, wba, wbr, wo, gt, sc, sh, gpost, gpre, *, seq, tm):
    t_rows, d = x2.shape
    tiles_per_seq = seq // tm
    row = lambda i: (i, 0)
    per_batch = lambda i: (i // tiles_per_seq, 0, 0)
    gw = GROUP_WIDTH
    attn_specs = [pl.BlockSpec((tm // dil, dil * gw), row) for _, dil in ATTN_GROUPS for _ in range(2)]
    n_slabs = 2 * len(ATTN_GROUPS) * (gw // LANES)
    return pl.pallas_call(
        _branch_kernel,
        out_shape=[jax.ShapeDtypeStruct((t_rows, d), F32), jax.ShapeDtypeStruct((t_rows, d), BF16)],
        grid=(t_rows // tm,),
        scratch_shapes=[pltpu.VMEM((n_slabs, tm, LANES), F32)],
        in_specs=attn_specs + [
            pl.BlockSpec((tm, 2 * d), row), pl.BlockSpec((tm, d), row), pl.BlockSpec((tm, d), row),
            _const_spec(wba.shape), _const_spec(wbr.shape), _const_spec(wo.shape),
            pl.BlockSpec((None, 1, d), per_batch), pl.BlockSpec((None, 1, d), per_batch),
            pl.BlockSpec((None, 1, d), per_batch),
            _const_spec((1, d)), _const_spec((1, d)),
        ],
        out_specs=[pl.BlockSpec((tm, d), row), pl.BlockSpec((tm, d), row)],
        compiler_params=pltpu.CompilerParams(
            dimension_semantics=("arbitrary",), vmem_limit_bytes=VMEM_LIMIT),
        name="branch",
    )(*attn_outs, gates, o_rwkv, x2, wba, wbr, wo, gt, sc, sh, gpost, gpre)


def _ffn_kernel(h_ref, x_ref, w1_ref, w2_ref, gt_ref, gpost_ref, o_ref, *, tf):
    h = h_ref[...]
    acc = jnp.zeros(x_ref.shape, F32)
    for j in range(w1_ref.shape[1] // tf):
        a = jnp.maximum(_dot(h, w1_ref[:, j * tf:(j + 1) * tf].astype(BF16)), 0.0)
        acc = acc + _dot((a * a).astype(BF16), w2_ref[j * tf:(j + 1) * tf, :].astype(BF16))
    o_ref[...] = x_ref[...] + gt_ref[...] * (_rms(acc) * gpost_ref[...])


def _ffn(h2, x1, w1, w2, gt, gpost, *, seq, tm, tf):
    t_rows, d = x1.shape
    tiles_per_seq = seq // tm
    row = lambda i: (i, 0)
    per_batch = lambda i: (i // tiles_per_seq, 0, 0)
    return pl.pallas_call(
        functools.partial(_ffn_kernel, tf=tf),
        out_shape=jax.ShapeDtypeStruct((t_rows, d), F32),
        grid=(t_rows // tm,),
        in_specs=[
            pl.BlockSpec((tm, d), row), pl.BlockSpec((tm, d), row),
            _const_spec(w1.shape), _const_spec(w2.shape),
            pl.BlockSpec((None, 1, d), per_batch), _const_spec((1, d)),
        ],
        out_specs=pl.BlockSpec((tm, d), row),
        compiler_params=pltpu.CompilerParams(
            dimension_semantics=("arbitrary",), vmem_limit_bytes=VMEM_LIMIT),
        name="ffn",
    )(h2, x1, w1, w2, gt, gpost)


def kernel(x, c, positions, ada_w, ada_b, norm_mix_pre, norm_mix_post, norm_ffn_pre, norm_ffn_post, w_in, shift_mu, decay_w0, decay_w2, iclr_a0, iclr_a2, gate_g2, k_k, k_a, r_k, gn_w, gn_b, w_branch, w_out, w_ff1, w_ff2):
    batch, seq, d = x.shape
    depth = ada_w.shape[0]
    t_rows = batch * seq
    n_groups = len(ATTN_GROUPS)
    attn_out_w = GROUP_WIDTH
    shift_w = shift_mu.shape[1]
    assert shift_w == 3 * d + DECAY_LORA + ICLR_LORA + GATE_LORA
    assert DECAY_LORA + ICLR_LORA == LANES and GATE_LORA == LANES
    tm = 512

    x2 = x.reshape(t_rows, d)
    pos_rows = positions.astype(F32).reshape(t_rows // tm, 1, tm)
    half = ROPE_DIM // 2
    inv_col = (ROPE_THETA ** (-jnp.arange(half, dtype=F32) * 2.0 / ROPE_DIM)).reshape(half, 1)
    c_pad = jnp.pad(c, ((0, 8 - batch), (0, 0)))
    vec = lambda p: p.reshape(1, -1)

    for l in range(depth):
        mod = _mod(c_pad, ada_w[l], ada_b[l].reshape(1, -1))[:batch]
        sh1, sc1, gt1, sh2, sc2, gt2 = [m.reshape(batch, 1, d) for m in jnp.split(mod, 6, axis=-1)]

        a0, a1, a2, rr, rk, rv, rl, gates = _inproj(
            x2, pos_rows, sc1, sh1, vec(norm_mix_pre[l]), inv_col, w_in[l].astype(BF16), vec(shift_mu[l]),
            seq=seq, tm=tm)

        attn_outs = _attn([a0, a1, a2], batch=batch, seq=seq)

        zeros = jnp.zeros((DECAY_LORA, d), BF16)
        w2p = jnp.concatenate([decay_w2[l].astype(BF16), zeros], axis=0)
        a2p = jnp.concatenate([zeros, iclr_a2[l].astype(BF16)], axis=0)
        o_rwkv = _wkv(rr, rk, rv, rl, vec(decay_w0[l]), w2p, vec(iclr_a0[l]), a2p,
                      gate_g2[l].astype(BF16), vec(k_k[l]), vec(k_a[l]), vec(r_k[l]),
                      vec(gn_w[l]), vec(gn_b[l]), batch=batch, seq=seq, tt=256)

        wb = w_branch[l]
        x1, h2 = _branch(attn_outs, gates, o_rwkv, x2, wb[:attn_out_w], wb[attn_out_w:],
                         w_out[l], gt1, sc2, sh2, vec(norm_mix_post[l]),
                         vec(norm_ffn_pre[l]), seq=seq, tm=tm)
        x2 = _ffn(h2, x1, w_ff1[l], w_ff2[l], gt2, vec(norm_ffn_post[l]),
                  seq=seq, tm=tm, tf=512)
    return x2.reshape(batch, seq, d)
```

```python
import functools

import jax
import jax.numpy as jnp
from jax import lax
from jax.experimental import pallas as pl
from jax.experimental.pallas import tpu as pltpu

F32 = jnp.float32
BF16 = jnp.bfloat16

HEAD_DIM = 64
ATTN_GROUPS = ((128, 1), (512, 4), (2048, 16))
ATTN_HEADS_PER_GROUP = 4
ATTN_BLOCK = 128
ATTN_BLOCKS_PER_STEP = 4
GROUP_WIDTH = ATTN_HEADS_PER_GROUP * HEAD_DIM
ROPE_THETA = 500000.0
ROPE_DIM = HEAD_DIM // 4
DECAY_LORA = 64
ICLR_LORA = 64
GATE_LORA = 128
NORM_EPS = 1e-6
GN_EPS = 64e-5
NEG_INF = -1e30

LANES = 128
WKV_CHUNK = 64
VMEM_LIMIT = 56 * 1024 * 1024


def _dot(a, b):
    return jnp.dot(a, b, preferred_element_type=F32)


def _dot_nt(a, b):
    return lax.dot_general(a, b, (((1,), (1,)), ((), ())), preferred_element_type=F32)


def _split2(x):
    hi = x.astype(BF16)
    lo = (x - hi.astype(F32)).astype(BF16)
    return hi, lo


def _sigmoid(x):
    return 1.0 / (1.0 + jnp.exp(-x))


def _rms(t):
    return t * lax.rsqrt(jnp.mean(t * t, axis=-1, keepdims=True) + NORM_EPS)


def _const_spec(shape):
    return pl.BlockSpec(shape, lambda *_: (0,) * len(shape), pipeline_mode=pl.Buffered(1))


def _mod_kernel(c_ref, w_ref, b_ref, o_ref):
    c = c_ref[...]
    a = c * _sigmoid(c)
    ah, al = _split2(a)
    wh, wl = _split2(w_ref[...])
    o_ref[...] = _dot(ah, wh) + _dot(ah, wl) + _dot(al, wh) + b_ref[...]


def _mod(c_pad, ada_w, ada_b):
    rows, d = c_pad.shape
    n = ada_w.shape[1]
    tn = 768
    return pl.pallas_call(
        _mod_kernel,
        out_shape=jax.ShapeDtypeStruct((rows, n), F32),
        grid=(n // tn,),
        in_specs=[
            pl.BlockSpec((rows, d), lambda j: (0, 0)),
            pl.BlockSpec((d, tn), lambda j: (0, j)),
            pl.BlockSpec((1, tn), lambda j: (0, j)),
        ],
        out_specs=pl.BlockSpec((rows, tn), lambda j: (0, j)),
        name="mod",
    )(c_pad, ada_w, ada_b)


def _inproj_kernel(x_ref, pos_ref, sc_ref, sh_ref, g_ref, inv_ref, spread_ref, w_ref, mu_ref,
                   a0_ref, a1_ref, a2_ref, rr_ref, rk_ref, rv_ref, rl_ref, gate_ref, carry_ref, dil_ref,
                   *, tiles_per_seq):
    i = pl.program_id(0)
    tm = x_ref.shape[0]
    d = x_ref.shape[1]
    h = _rms(x_ref[...]) * g_ref[...]
    hb = (h * (1.0 + sc_ref[...]) + sh_ref[...]).astype(BF16)

    half = ROPE_DIM // 2
    ang = inv_ref[...] * pos_ref[...]
    cs_t = jnp.concatenate([jnp.cos(ang), jnp.sin(ang)], axis=0).T
    tab = _dot(jnp.concatenate(_split2(cs_t), axis=1), spread_ref[...])
    lane = lax.broadcasted_iota(jnp.int32, (1, LANES), 1) % HEAD_DIM
    cc = tab[:, 0:LANES] + jnp.where(lane < ROPE_DIM, 0.0, 1.0)
    s_lo = tab[:, LANES:2 * LANES]
    s_hi = tab[:, 2 * LANES:3 * LANES]

    def rope(t):
        return t * cc + pltpu.roll(t, LANES - half, 1) * s_lo + pltpu.roll(t, half, 1) * s_hi

    a_refs = (a0_ref, a1_ref, a2_ref)
    aw = len(ATTN_GROUPS) * GROUP_WIDTH
    gw = GROUP_WIDTH
    slab = 0
    for which in range(3):
        t = _dot(hb, w_ref[:, which * aw:(which + 1) * aw])
        for g, (_, dil) in enumerate(ATTN_GROUPS):
            halves = [t[:, g * gw + s * LANES:g * gw + (s + 1) * LANES] for s in range(gw // LANES)]
            if which < 2:
                halves = [rope(hv) for hv in halves]
            if which == 0:
                halves = [hv * (HEAD_DIM ** -0.5) for hv in halves]
            for s, hv in enumerate(halves):
                if dil == 1:
                    a_refs[g][:, which * gw + s * LANES:which * gw + (s + 1) * LANES] = hv.astype(BF16)
                    continue
                dil_ref[slab] = hv
                for r in range(dil):
                    col = r * 3 * gw + which * gw + s * LANES
                    a_refs[g][:, col:col + LANES] = dil_ref[slab, pl.ds(r, tm // dil, stride=dil), :].astype(BF16)
                slab += 1

    first = (i % tiles_per_seq) == 0
    row0 = lax.broadcasted_iota(jnp.int32, (tm, 1), 0) == 0
    shift_col = 3 * aw
    gate_col = shift_col + mu_ref.shape[1]
    col = 0
    for ref in (rr_ref, rk_ref, rv_ref, rl_ref):
        n = ref.shape[1]
        u = _dot(hb, w_ref[:, shift_col + col:shift_col + col + n])
        carry = jnp.where(first, 0.0, carry_ref[:, col:col + n])
        prev = jnp.where(row0, carry, pltpu.roll(u, 1, 0))
        carry_ref[:, col:col + n] = u[tm - 1:tm, :]
        ref[...] = (u + (prev - u) * mu_ref[:, col:col + n]).astype(ref.dtype)
        col += n

    for j in range(2):
        gj = _dot(hb, w_ref[:, gate_col + j * d:gate_col + (j + 1) * d])
        gate_ref[:, j * d:(j + 1) * d] = _sigmoid(gj).astype(BF16)


def _rope_spread():
    half = ROPE_DIM // 2
    cols = jnp.arange(3 * LANES)
    block, in_head = cols // LANES, cols % HEAD_DIM
    freq = in_head % half
    src_row = jnp.where(block == 0, freq, half + freq)
    sign = jnp.where(block == 1, -1.0, 1.0)
    used = jnp.where(block == 0, in_head < ROPE_DIM,
                     jnp.where(block == 1, in_head < half, (in_head >= half) & (in_head < ROPE_DIM)))
    one_piece = jnp.where((jnp.arange(2 * half)[:, None] == src_row[None, :]) & used[None, :], sign[None, :], 0.0)
    return jnp.concatenate([one_piece, one_piece], axis=0).astype(BF16)


def _inproj(x2, pos_rows, sc, sh, gain, inv_col, w, mu, *, seq, tm):
    t_rows, d = x2.shape
    tiles_per_seq = seq // tm
    aw = len(ATTN_GROUPS) * GROUP_WIDTH
    shift_w = mu.shape[1]
    assert w.shape[1] == 3 * aw + shift_w + 2 * d
    lora_w = shift_w - 3 * d
    row = lambda i: (i, 0)
    per_batch = lambda i: (i // tiles_per_seq, 0, 0)
    dils = [dil for _, dil in ATTN_GROUPS]
    out_shape = (
        [jax.ShapeDtypeStruct((t_rows // dil, dil * aw), BF16) for dil in dils]
        + [jax.ShapeDtypeStruct((t_rows, d), BF16)] * 3
        + [jax.ShapeDtypeStruct((t_rows, lora_w), F32), jax.ShapeDtypeStruct((t_rows, 2 * d), BF16)]
    )
    out_specs = (
        [pl.BlockSpec((tm // dil, dil * aw), row) for dil in dils]
        + [pl.BlockSpec((tm, d), row)] * 3
        + [pl.BlockSpec((tm, lora_w), row), pl.BlockSpec((tm, 2 * d), row)]
    )
    n_slabs = sum(3 * GROUP_WIDTH // LANES for dil in dils if dil > 1)
    spread = _rope_spread()
    return pl.pallas_call(
        functools.partial(_inproj_kernel, tiles_per_seq=tiles_per_seq),
        out_shape=out_shape,
        grid=(t_rows // tm,),
        in_specs=[
            pl.BlockSpec((tm, d), row),
            pl.BlockSpec((None, 1, tm), lambda i: (i, 0, 0)),
            pl.BlockSpec((None, 1, d), per_batch),
            pl.BlockSpec((None, 1, d), per_batch),
            _const_spec((1, d)),
            _const_spec(inv_col.shape),
            _const_spec(spread.shape),
            _const_spec(w.shape),
            _const_spec((1, shift_w)),
        ],
        out_specs=out_specs,
        scratch_shapes=[pltpu.VMEM((1, shift_w), F32), pltpu.VMEM((n_slabs, tm, LANES), F32)],
        compiler_params=pltpu.CompilerParams(
            dimension_semantics=("arbitrary",), vmem_limit_bytes=VMEM_LIMIT),
        name="inproj",
    )(x2, pos_rows, sc, sh, gain, inv_col, spread, w, mu)


def _attn_kernel(*refs, steps_per_subseq):
    n_groups = len(steps_per_subseq)
    in_refs, out_refs = refs[:2 * n_groups], refs[2 * n_groups:]
    m = pl.program_id(1)
    blk = ATTN_BLOCK
    gw = GROUP_WIDTH
    nh = ATTN_HEADS_PER_GROUP
    row = lax.broadcasted_iota(jnp.int32, (nh * blk, blk), 0) % blk
    colk = lax.broadcasted_iota(jnp.int32, (nh * blk, blk), 1)
    mask_cur = colk <= row
    mask_prev_band = colk >= row
    head_of_lane = lax.broadcasted_iota(jnp.int32, (1, gw), 1) // HEAD_DIM

    probs = []
    for g in range(n_groups):
        cur, prev = in_refs[2 * g], in_refs[2 * g + 1]
        first = mask_prev_band & ((m % steps_per_subseq[g]) > 0)
        probs.append((g, cur, slice(0, blk), prev, slice(0, blk), first))
        for nb in range(1, ATTN_BLOCKS_PER_STEP):
            probs.append((g, cur, slice(nb * blk, (nb + 1) * blk), cur, slice((nb - 1) * blk, nb * blk),
                          mask_prev_band))
    idx = range(len(probs))
    q4 = []
    for _, cur, rows, _, _, _ in probs:
        q = cur[rows, 0:gw]
        zq = jnp.zeros_like(q)
        q4.append(jnp.concatenate([jnp.where(head_of_lane == hd, q, zq) for hd in range(nh)], axis=0))
    sc = [jnp.where(mask_cur, _dot_nt(q4[i], p[1][p[2], gw:2 * gw]), NEG_INF) for i, p in zip(idx, probs)]
    sp = [jnp.where(p[5], _dot_nt(q4[i], p[3][p[4], gw:2 * gw]), NEG_INF) for i, p in zip(idx, probs)]
    mx = [jnp.max(jnp.maximum(sc[i], sp[i]), axis=-1, keepdims=True) for i in idx]
    pc = [jnp.exp(sc[i] - mx[i]) for i in idx]
    pp = [jnp.exp(sp[i] - mx[i]) for i in idx]
    den = [jnp.sum(pc[i] + pp[i], axis=-1, keepdims=True) for i in idx]
    pcb = [pc[i].astype(BF16) for i in idx]
    ppb = [pp[i].astype(BF16) for i in idx]
    low_half = lax.broadcasted_iota(jnp.int32, (1, LANES), 1) < HEAD_DIM
    pv = []
    for i, p in zip(idx, probs):
        per_pair = []
        for j in range(nh // 2):
            rows2 = slice(2 * j * blk, (2 * j + 2) * blk)
            vcol = slice(2 * gw + j * LANES, 2 * gw + (j + 1) * LANES)
            per_pair.append(_dot(pcb[i][rows2], p[1][p[2], vcol]) + _dot(ppb[i][rows2], p[3][p[4], vcol]))
        pv.append(per_pair)
    for i, (g, _, rows, _, _, _) in zip(idx, probs):
        inv_den = 1.0 / den[i]
        l4 = mx[i] + jnp.log(den[i])
        o_pairs, l_pairs = [], []
        for j in range(nh // 2):
            lo_rows = slice(2 * j * blk, (2 * j + 1) * blk)
            hi_rows = slice((2 * j + 1) * blk, (2 * j + 2) * blk)
            scaled = pv[i][j] * inv_den[2 * j * blk:(2 * j + 2) * blk]
            o_pairs.append(jnp.where(low_half, scaled[0:blk], scaled[blk:2 * blk]))
            l_pairs.append(jnp.where(low_half, l4[lo_rows], l4[hi_rows]))
        out_refs[2 * g][rows, :] = jnp.concatenate(o_pairs, axis=1).astype(BF16)
        out_refs[2 * g + 1][rows, :] = jnp.concatenate(l_pairs, axis=1)


def _attn(a_list, *, batch, seq):
    assert all(w // dil == ATTN_BLOCK for w, dil in ATTN_GROUPS)
    blk, gw = ATTN_BLOCK, GROUP_WIDTH
    per_step = ATTN_BLOCKS_PER_STEP
    n_steps = seq // (per_step * blk)
    ins, in_specs, out_shape, out_specs, nss = [], [], [], [], []
    for a, (_, dil) in zip(a_list, ATTN_GROUPS):
        assert seq % (dil * per_step * blk) == 0
        ns = n_steps // dil
        nss.append(ns)
        view = a.reshape(batch, seq // dil, dil * 3 * gw)
        cur = lambda b, m, ns=ns: (b, m % ns, m // ns)
        prev = lambda b, m, ns=ns: (b, jnp.maximum(per_step * (m % ns) - 1, 0), m // ns)
        ins += [view, view]
        in_specs += [pl.BlockSpec((None, per_step * blk, 3 * gw), cur), pl.BlockSpec((None, blk, 3 * gw), prev)]
        out_shape += [jax.ShapeDtypeStruct((batch, seq // dil, dil * gw), BF16),
                      jax.ShapeDtypeStruct((batch, seq // dil, dil * gw), F32)]
        out_specs += [pl.BlockSpec((None, per_step * blk, gw), cur)] * 2
    outs = pl.pallas_call(
        functools.partial(_attn_kernel, steps_per_subseq=tuple(nss)),
        out_shape=out_shape,
        grid=(batch, n_steps),
        in_specs=in_specs,
        out_specs=out_specs,
        compiler_params=pltpu.CompilerParams(dimension_semantics=("arbitrary", "arbitrary")),
        name="attn",
    )(*ins)
    return [o.reshape(-1, o.shape[-1]) for o in outs]


def _seg_sum(z, seg_ones):
    rows, c = z.shape
    n = c // LANES
    zs = jnp.concatenate([z[:, j * LANES:(j + 1) * LANES] for j in range(n)], axis=0).astype(BF16)
    out = _dot(zs, seg_ones)
    return jnp.concatenate([out[j * rows:(j + 1) * rows] for j in range(n)], axis=1)


def _wkv_kernel(rr_ref, rk_ref, rv_ref, rl_ref, w0_ref, w2_ref, a0_ref, a2_ref, g2_ref, kk_ref, ka_ref,
                rkp_ref, gnw_ref, gnb_ref, o_ref,
                state_ref, at_ref, rt_ref, bt_ref, kt_ref, v_ref, gam_ref, bonus_ref, g_ref):
    tt, c = rr_ref.shape
    n_pairs = c // LANES
    ck = WKV_CHUNK
    n_chunks = tt // ck
    staged = (at_ref, rt_ref, bt_ref, kt_ref, v_ref, gam_ref, bonus_ref, g_ref)

    @pl.when(pl.program_id(1) == 0)
    def _():
        state_ref[...] = jnp.zeros_like(state_ref)
        for ref in staged:
            ref[...] = jnp.zeros_like(ref)

    at_n, rt_n, bt_n, kt_n, v_n, gam_n, bonus_p, g_p = [ref[...] for ref in staged]

    lane = lax.broadcasted_iota(jnp.int32, (LANES, LANES), 1)
    rowi = lax.broadcasted_iota(jnp.int32, (LANES, LANES), 0)
    seg_ones = jnp.where(lane // HEAD_DIM == rowi // HEAD_DIM, 1.0, 0.0).astype(BF16)

    def stage_lora():
        lora = rl_ref[...]
        dwa = lora[:, 0:LANES]
        dg = lora[:, LANES:2 * LANES]
        zw = w0_ref[...] + _dot(jnp.tanh(dwa).astype(BF16), w2_ref[...])
        za = a0_ref[...] + _dot(dwa.astype(BF16), a2_ref[...])
        return zw, za, _dot(_sigmoid(dg).astype(BF16), g2_ref[...])

    tri2 = (lax.broadcasted_iota(jnp.int32, (ck, 2 * ck), 1) % ck
            <= lax.broadcasted_iota(jnp.int32, (ck, 2 * ck), 0)).astype(BF16)

    def stage_chunk(ci, zw, za):
        rows = slice(ci * ck, (ci + 1) * ck)
        r = rr_ref[rows, :].astype(F32)
        k = rk_ref[rows, :].astype(F32)
        v = rv_ref[rows, :]
        zw = zw[rows]
        w_log = -(jnp.maximum(-zw, 0.0) + jnp.log(1.0 + jnp.exp(-jnp.abs(zw)))) - 0.5
        lw = -jnp.exp(w_log)
        a = _sigmoid(za[rows])
        kk = k * kk_ref[...]
        kk = kk * lax.rsqrt(jnp.maximum(_seg_sum(kk * kk, seg_ones), 1e-24))
        kmod = k * (1.0 + (a - 1.0) * ka_ref[...])
        bonus = _seg_sum(r * kmod * rkp_ref[...], seg_ones) * v.astype(F32)
        hi, lo = _split2(lw)
        cum = _dot(tri2, jnp.concatenate([hi, lo], axis=0))
        e_in = jnp.exp(cum)
        e_neg = jnp.exp(-cum)
        return (
            (-kk * jnp.exp(cum - lw)).astype(BF16),
            (r * e_in).astype(BF16),
            (kk * a * e_neg).astype(BF16),
            (kmod * e_neg).astype(BF16),
            v,
            e_in[ck - 8:ck],
            bonus,
        )

    t_idx = lax.broadcasted_iota(jnp.int32, (ck, LANES), 0)
    i_idx = lax.broadcasted_iota(jnp.int32, (ck, LANES), 1) % ck
    strict = i_idx < t_idx
    incl = i_idx <= t_idx
    lo_mask = lax.broadcasted_iota(jnp.int32, (1, LANES), 1) < HEAD_DIM
    eye_cat = jnp.where(i_idx == t_idx, 1.0, 0.0)
    bk_row_head = lax.broadcasted_iota(jnp.int32, (LANES, 2 * LANES), 0) // HEAD_DIM
    bk_col_head = (lax.broadcasted_iota(jnp.int32, (LANES, 2 * LANES), 1) // ck) % 2
    bk_mask = bk_row_head == bk_col_head

    def entries(t):
        return jnp.stack([t[ci * ck:(ci + 1) * ck, p * LANES:(p + 1) * LANES]
                          for ci in range(n_chunks) for p in range(n_pairs)])

    def stack_heads(t):
        z = jnp.zeros_like(t)
        return jnp.concatenate([jnp.where(lo_mask, t, z), jnp.where(lo_mask, z, t)], axis=1).astype(BF16)

    def bmm(a_, b_):
        return jnp.einsum('pmk,pkn->pmn', a_, b_, preferred_element_type=F32)

    at = entries(at_n)
    rt = entries(rt_n)
    bt = entries(bt_n)
    kt = entries(kt_n)
    bk_t = jnp.swapaxes(jnp.concatenate([bt, bt, kt, kt], axis=1), 1, 2)
    bk_t = jnp.where(bk_mask, bk_t, jnp.zeros_like(bk_t))
    vw = stack_heads(entries(v_n))
    gcol = jnp.swapaxes(
        jnp.stack([gam_n[ci * 8:(ci + 1) * 8, p * LANES:(p + 1) * LANES]
                   for ci in range(n_chunks) for p in range(n_pairs)]), 1, 2)[:, :, 7:8]

    gm = bmm(jnp.concatenate([at, rt], axis=1), bk_t)
    a_ab = jnp.where(strict, gm[:, :ck, :LANES], 0.0)
    a_ak = jnp.where(strict, gm[:, :ck, LANES:], 0.0).astype(BF16)
    a_rb = jnp.where(incl, gm[:, ck:, :LANES], 0.0).astype(BF16)
    a_rk = jnp.where(incl, gm[:, ck:, LANES:], 0.0).astype(BF16)
    zw_t, za_t, g_new = stage_lora()
    new_chunks = []
    stage_order = iter(range(n_chunks))

    def stage_next():
        ci = next(stage_order, None)
        if ci is not None:
            new_chunks.append(stage_chunk(ci, zw_t, za_t))

    stage_next()
    tm_ = eye_cat + a_ab
    pw = bmm(a_ab.astype(BF16), stack_heads(a_ab))
    for step in range(4):
        both = bmm(jnp.concatenate([tm_, pw], axis=1).astype(BF16), stack_heads(pw))
        tm_ = tm_ + both[:, :ck, :]
        pw = both[:, ck:, :]
        if step % 2 == 0:
            stage_next()
    tm_ = tm_ + bmm(tm_.astype(BF16), stack_heads(pw))
    t_cat = tm_.astype(BF16)
    by_v = bmm(jnp.concatenate([a_ak, a_rk, bk_t[:, :, LANES:]], axis=1), vw)
    ta = bmm(t_cat, stack_heads(at))
    for _ in range(n_chunks):
        stage_next()
    tx = bmm(t_cat, stack_heads(by_v[:, :ck, :]))
    ra = rt + bmm(a_rb, stack_heads(ta))
    yc = by_v[:, ck:2 * ck, :] + bmm(a_rb, stack_heads(tx))
    dk = by_v[:, 2 * ck:, :]
    new_staged = [jnp.concatenate(parts, axis=0) for parts in zip(*new_chunks)] + [g_new]
    tara = jnp.concatenate([ta, ra], axis=1).astype(BF16)
    b_t = bk_t[:, :, :LANES]

    s = state_ref[...]
    y_rows = []
    for ci in range(n_chunks):
        es = slice(ci * n_pairs, (ci + 1) * n_pairs)
        o1 = bmm(tara[es], s.astype(BF16))
        u = o1[:, :ck, :] + tx[es]
        y = o1[:, ck:, :] + yc[es]
        y_rows.append(jnp.concatenate([y[p] for p in range(n_pairs)], axis=1))
        s = (s + bmm(b_t[es], stack_heads(u)) + dk[es]) * gcol[es]
    state_ref[...] = s

    y = jnp.concatenate(y_rows, axis=0)
    inv_n = 1.0 / HEAD_DIM
    mean = _seg_sum(y, seg_ones) * inv_n
    ycen = y - mean
    var = _seg_sum(ycen * ycen, seg_ones) * inv_n
    yn = ycen * lax.rsqrt(var + GN_EPS) * gnw_ref[...] + gnb_ref[...]
    o_ref[...] = ((yn + bonus_p) * g_p).astype(o_ref.dtype)

    for ref, val in zip(staged, new_staged):
        ref[...] = val


def _wkv(rr, rk, rv, rl, w0, w2p, a0, a2p, g2, k_k, k_a, r_k, gn_w, gn_b, *, batch, seq, tt):
    t_rows, c = rr.shape
    steps = seq // tt
    row_in = lambda b, j: (b * steps + jnp.minimum(j, steps - 1), 0)
    row_out = lambda b, j: (b * steps + jnp.maximum(j - 1, 0), 0)
    vec = _const_spec((1, c))
    n_gam = (tt // WKV_CHUNK) * 8
    return pl.pallas_call(
        _wkv_kernel,
        out_shape=jax.ShapeDtypeStruct((t_rows, c), BF16),
        grid=(batch, steps + 1),
        in_specs=[
            pl.BlockSpec((tt, c), row_in), pl.BlockSpec((tt, c), row_in), pl.BlockSpec((tt, c), row_in),
            pl.BlockSpec((tt, rl.shape[1]), row_in),
            vec, _const_spec(w2p.shape), vec, _const_spec(a2p.shape), _const_spec(g2.shape),
            vec, vec, vec, vec, vec,
        ],
        out_specs=pl.BlockSpec((tt, c), row_out),
        scratch_shapes=[pltpu.VMEM((c // LANES, LANES, LANES), F32)]
        + [pltpu.VMEM((tt, c), BF16)] * 5
        + [pltpu.VMEM((n_gam, c), F32), pltpu.VMEM((tt, c), F32), pltpu.VMEM((tt, c), F32)],
        compiler_params=pltpu.CompilerParams(
            dimension_semantics=("arbitrary", "arbitrary"), vmem_limit_bytes=VMEM_LIMIT),
        name="wkv",
    )(rr, rk, rv, rl, w0, w2p, a0, a2p, g2, k_k, k_a, r_k, gn_w, gn_b)


def _branch_kernel(o0_ref, l0_ref, o1_ref, l1_ref, o2_ref, l2_ref, gate_ref, orw_ref, x_ref,
                   wba_ref, wbr_ref, wo_ref, gt_ref, sc_ref, sh_ref, gpost_ref, gpre_ref,
                   x1_ref, h2_ref, nat_ref):
    tm, d = x_ref.shape
    gw = GROUP_WIDTH

    def natural(ref, dil, slab):
        if dil == 1:
            return ref[...].astype(F32)
        for r in range(dil):
            for s in range(gw // LANES):
                nat_ref[slab + s, pl.ds(r, tm // dil, stride=dil), :] = (
                    ref[:, r * gw + s * LANES:r * gw + (s + 1) * LANES].astype(F32))
        return jnp.concatenate([nat_ref[slab + s] for s in range(gw // LANES)], axis=1)

    dils = [dil for _, dil in ATTN_GROUPS]
    per_group = gw // LANES
    o0, o1, o2 = [natural(ref, dil, 2 * g * per_group)
                  for g, (ref, dil) in enumerate(zip((o0_ref, o1_ref, o2_ref), dils))]
    l0, l1, l2 = [natural(ref, dil, (2 * g + 1) * per_group)
                  for g, (ref, dil) in enumerate(zip((l0_ref, l1_ref, l2_ref), dils))]
    mx = jnp.maximum(jnp.maximum(l0, l1), l2)
    e0, e1, e2 = jnp.exp(l0 - mx), jnp.exp(l1 - mx), jnp.exp(l2 - mx)
    o_attn = (e0 * o0 + e1 * o1 + e2 * o2) / (e0 + e1 + e2)
    ya = _dot(o_attn.astype(BF16), wba_ref[...].astype(BF16))
    yr = _dot(orw_ref[...], wbr_ref[...].astype(BF16))
    merged = gate_ref[:, 0:d].astype(F32) * ya + gate_ref[:, d:2 * d].astype(F32) * yr
    mix = _dot(merged.astype(BF16), wo_ref[...].astype(BF16))
    x1 = x_ref[...] + gt_ref[...] * (_rms(mix) * gpost_ref[...])
    x1_ref[...] = x1
    h2 = _rms(x1) * gpre_ref[...]
    h2_ref[...] = (h2 * (1.0 + sc_ref[...]) + sh_ref[...]).astype(BF16)


def _branch(attn_outs, gates, o_rwkv, x2, wba, wbr, wo, gt, sc, sh, gpost, gpre, *, seq, tm):
    t_rows, d = x2.shape
    tiles_per_seq = seq // tm
    row = lambda i: (i, 0)
    per_batch = lambda i: (i // tiles_per_seq, 0, 0)
    gw = GROUP_WIDTH
    attn_specs = [pl.BlockSpec((tm // dil, dil * gw), row) for _, dil in ATTN_GROUPS for _ in range(2)]
    n_slabs = 2 * len(ATTN_GROUPS) * (gw // LANES)
    return pl.pallas_call(
        _branch_kernel,
        out_shape=[jax.ShapeDtypeStruct((t_rows, d), F32), jax.ShapeDtypeStruct((t_rows, d), BF16)],
        grid=(t_rows // tm,),
        scratch_shapes=[pltpu.VMEM((n_slabs, tm, LANES), F32)],
        in_specs=attn_specs + [
            pl.BlockSpec((tm, 2 * d), row), pl.BlockSpec((tm, d), row), pl.BlockSpec((tm, d), row),
            _const_spec(wba.shape), _const_spec(wbr.shape), _const_spec(wo.shape),
            pl.BlockSpec((None, 1, d), per_batch), pl.BlockSpec((None, 1, d), per_batch),
            pl.BlockSpec((None, 1, d), per_batch),
            _const_spec((1, d)), _const_spec((1, d)),
        ],
        out_specs=[pl.BlockSpec((tm, d), row), pl.BlockSpec((tm, d), row)],
        compiler_params=pltpu.CompilerParams(
            dimension_semantics=("arbitrary",), vmem_limit_bytes=VMEM_LIMIT),
        name="branch",
    )(*attn_outs, gates, o_rwkv, x2, wba, wbr, wo, gt, sc, sh, gpost, gpre)


def _ffn_kernel(h_ref, x_ref, w1_ref, w2_ref, gt_ref, gpost_ref, o_ref, *, tf):
    h = h_ref[...]
    acc = jnp.zeros(x_ref.shape, F32)
    for j in range(w1_ref.shape[1] // tf):
        a = jnp.maximum(_dot(h, w1_ref[:, j * tf:(j + 1) * tf].astype(BF16)), 0.0)
        acc = acc + _dot((a * a).astype(BF16), w2_ref[j * tf:(j + 1) * tf, :].astype(BF16))
    o_ref[...] = x_ref[...] + gt_ref[...] * (_rms(acc) * gpost_ref[...])


def _ffn(h2, x1, w1, w2, gt, gpost, *, seq, tm, tf):
    t_rows, d = x1.shape
    tiles_per_seq = seq // tm
    row = lambda i: (i, 0)
    per_batch = lambda i: (i // tiles_per_seq, 0, 0)
    return pl.pallas_call(
        functools.partial(_ffn_kernel, tf=tf),
        out_shape=jax.ShapeDtypeStruct((t_rows, d), F32),
        grid=(t_rows // tm,),
        in_specs=[
            pl.BlockSpec((tm, d), row), pl.BlockSpec((tm, d), row),
            _const_spec(w1.shape), _const_spec(w2.shape),
            pl.BlockSpec((None, 1, d), per_batch), _const_spec((1, d)),
        ],
        out_specs=pl.BlockSpec((tm, d), row),
        compiler_params=pltpu.CompilerParams(
            dimension_semantics=("arbitrary",), vmem_limit_bytes=VMEM_LIMIT),
        name="ffn",
    )(h2, x1, w1, w2, gt, gpost)


def kernel(x, c, positions, ada_w, ada_b, norm_mix_pre, norm_mix_post, norm_ffn_pre, norm_ffn_post, w_in, shift_mu, decay_w0, decay_w2, iclr_a0, iclr_a2, gate_g2, k_k, k_a, r_k, gn_w, gn_b, w_branch, w_out, w_ff1, w_ff2):
    batch, seq, d = x.shape
    depth = ada_w.shape[0]
    t_rows = batch * seq
    n_groups = len(ATTN_GROUPS)
    attn_out_w = GROUP_WIDTH
    shift_w = shift_mu.shape[1]
    assert shift_w == 3 * d + DECAY_LORA + ICLR_LORA + GATE_LORA
    assert DECAY_LORA + ICLR_LORA == LANES and GATE_LORA == LANES
    tm = 512

    x2 = x.reshape(t_rows, d)
    pos_rows = positions.astype(F32).reshape(t_rows // tm, 1, tm)
    half = ROPE_DIM // 2
    inv_col = (ROPE_THETA ** (-jnp.arange(half, dtype=F32) * 2.0 / ROPE_DIM)).reshape(half, 1)
    c_pad = jnp.pad(c, ((0, 8 - batch), (0, 0)))
    vec = lambda p: p.reshape(1, -1)

    for l in range(depth):
        mod = _mod(c_pad, ada_w[l], ada_b[l].reshape(1, -1))[:batch]
        sh1, sc1, gt1, sh2, sc2, gt2 = [m.reshape(batch, 1, d) for m in jnp.split(mod, 6, axis=-1)]

        a0, a1, a2, rr, rk, rv, rl, gates = _inproj(
            x2, pos_rows, sc1, sh1, vec(norm_mix_pre[l]), inv_col, w_in[l].astype(BF16), vec(shift_mu[l]),
            seq=seq, tm=tm)

        attn_outs = _attn([a0, a1, a2], batch=batch, seq=seq)

        zeros = jnp.zeros((DECAY_LORA, d), BF16)
        w2p = jnp.concatenate([decay_w2[l].astype(BF16), zeros], axis=0)
        a2p = jnp.concatenate([zeros, iclr_a2[l].astype(BF16)], axis=0)
        o_rwkv = _wkv(rr, rk, rv, rl, vec(decay_w0[l]), w2p, vec(iclr_a0[l]), a2p,
                      gate_g2[l].astype(BF16), vec(k_k[l]), vec(k_a[l]), vec(r_k[l]),
                      vec(gn_w[l]), vec(gn_b[l]), batch=batch, seq=seq, tt=256)

        wb = w_branch[l]
        x1, h2 = _branch(attn_outs, gates, o_rwkv, x2, wb[:attn_out_w], wb[attn_out_w:],
                         w_out[l], gt1, sc2, sh2, vec(norm_mix_post[l]),
                         vec(norm_ffn_pre[l]), seq=seq, tm=tm)
        x2 = _ffn(h2, x1, w_ff1[l], w_ff2[l], gt2, vec(norm_ffn_post[l]),
                  seq=seq, tm=tm, tf=512)
    return x2.reshape(batch, seq, d)
```

```python
import functools

import jax
import jax.numpy as jnp
from jax import lax
from jax.experimental import pallas as pl
from jax.experimental.pallas import tpu as pltpu

F32 = jnp.float32
BF16 = jnp.bfloat16

HEAD_DIM = 64
ATTN_GROUPS = ((128, 1), (512, 4), (2048, 16))
ATTN_HEADS_PER_GROUP = 4
ATTN_BLOCK = 128
ATTN_BLOCKS_PER_STEP = 4
GROUP_WIDTH = ATTN_HEADS_PER_GROUP * HEAD_DIM
ROPE_THETA = 500000.0
ROPE_DIM = HEAD_DIM // 4
DECAY_LORA = 64
ICLR_LORA = 64
GATE_LORA = 128
NORM_EPS = 1e-6
GN_EPS = 64e-5
NEG_INF = -1e30

LANES = 128
WKV_CHUNK = 64
VMEM_LIMIT = 56 * 1024 * 1024


def _dot(a, b):
    return jnp.dot(a, b, preferred_element_type=F32)


def _dot_nt(a, b):
    return lax.dot_general(a, b, (((1,), (1,)), ((), ())), preferred_element_type=F32)


def _split2(x):
    hi = x.astype(BF16)
    lo = (x - hi.astype(F32)).astype(BF16)
    return hi, lo


def _sigmoid(x):
    return 1.0 / (1.0 + jnp.exp(-x))


def _rms(t):
    return t * lax.rsqrt(jnp.mean(t * t, axis=-1, keepdims=True) + NORM_EPS)


def _const_spec(shape):
    return pl.BlockSpec(shape, lambda *_: (0,) * len(shape), pipeline_mode=pl.Buffered(1))


def _mod_kernel(c_ref, w_ref, b_ref, o_ref):
    c = c_ref[...]
    a = c * _sigmoid(c)
    ah, al = _split2(a)
    wh, wl = _split2(w_ref[...])
    o_ref[...] = _dot(ah, wh) + _dot(ah, wl) + _dot(al, wh) + b_ref[...]


def _mod(c_pad, ada_w, ada_b):
    rows, d = c_pad.shape
    n = ada_w.shape[1]
    tn = 768
    return pl.pallas_call(
        _mod_kernel,
        out_shape=jax.ShapeDtypeStruct((rows, n), F32),
        grid=(n // tn,),
        in_specs=[
            pl.BlockSpec((rows, d), lambda j: (0, 0)),
            pl.BlockSpec((d, tn), lambda j: (0, j)),
            pl.BlockSpec((1, tn), lambda j: (0, j)),
        ],
        out_specs=pl.BlockSpec((rows, tn), lambda j: (0, j)),
        name="mod",
    )(c_pad, ada_w, ada_b)


def _inproj_kernel(x_ref, pos_ref, sc_ref, sh_ref, g_ref, inv_ref, spread_ref, w_ref, mu_ref,
                   a0_ref, a1_ref, a2_ref, rr_ref, rk_ref, rv_ref, rl_ref, gate_ref, carry_ref, dil_ref,
                   *, tiles_per_seq):
    i = pl.program_id(0)
    tm = x_ref.shape[0]
    d = x_ref.shape[1]
    h = _rms(x_ref[...]) * g_ref[...]
    hb = (h * (1.0 + sc_ref[...]) + sh_ref[...]).astype(BF16)

    half = ROPE_DIM // 2
    ang = inv_ref[...] * pos_ref[...]
    cs_t = jnp.concatenate([jnp.cos(ang), jnp.sin(ang)], axis=0).T
    tab = _dot(jnp.concatenate(_split2(cs_t), axis=1), spread_ref[...])
    lane = lax.broadcasted_iota(jnp.int32, (1, LANES), 1) % HEAD_DIM
    cc = tab[:, 0:LANES] + jnp.where(lane < ROPE_DIM, 0.0, 1.0)
    s_lo = tab[:, LANES:2 * LANES]
    s_hi = tab[:, 2 * LANES:3 * LANES]

    def rope(t):
        return t * cc + pltpu.roll(t, LANES - half, 1) * s_lo + pltpu.roll(t, half, 1) * s_hi

    a_refs = (a0_ref, a1_ref, a2_ref)
    aw = len(ATTN_GROUPS) * GROUP_WIDTH
    gw = GROUP_WIDTH
    slab = 0
    for which in range(3):
        t = _dot(hb, w_ref[:, which * aw:(which + 1) * aw])
        for g, (_, dil) in enumerate(ATTN_GROUPS):
            halves = [t[:, g * gw + s * LANES:g * gw + (s + 1) * LANES] for s in range(gw // LANES)]
            if which < 2:
                halves = [rope(hv) for hv in halves]
            if which == 0:
                halves = [hv * (HEAD_DIM ** -0.5) for hv in halves]
            for s, hv in enumerate(halves):
                if dil == 1:
                    a_refs[g][:, which * gw + s * LANES:which * gw + (s + 1) * LANES] = hv.astype(BF16)
                    continue
                dil_ref[slab] = hv
                for r in range(dil):
                    col = r * 3 * gw + which * gw + s * LANES
                    a_refs[g][:, col:col + LANES] = dil_ref[slab, pl.ds(r, tm // dil, stride=dil), :].astype(BF16)
                slab += 1

    first = (i % tiles_per_seq) == 0
    row0 = lax.broadcasted_iota(jnp.int32, (tm, 1), 0) == 0
    shift_col = 3 * aw
    gate_col = shift_col + mu_ref.shape[1]
    col = 0
    for ref in (rr_ref, rk_ref, rv_ref, rl_ref):
        n = ref.shape[1]
        u = _dot(hb, w_ref[:, shift_col + col:shift_col + col + n])
        carry = jnp.where(first, 0.0, carry_ref[:, col:col + n])
        prev = jnp.where(row0, carry, pltpu.roll(u, 1, 0))
        carry_ref[:, col:col + n] = u[tm - 1:tm, :]
        ref[...] = (u + (prev - u) * mu_ref[:, col:col + n]).astype(ref.dtype)
        col += n

    for j in range(2):
        gj = _dot(hb, w_ref[:, gate_col + j * d:gate_col + (j + 1) * d])
        gate_ref[:, j * d:(j + 1) * d] = _sigmoid(gj).astype(BF16)


def _rope_spread():
    half = ROPE_DIM // 2
    cols = jnp.arange(3 * LANES)
    block, in_head = cols // LANES, cols % HEAD_DIM
    freq = in_head % half
    src_row = jnp.where(block == 0, freq, half + freq)
    sign = jnp.where(block == 1, -1.0, 1.0)
    used = jnp.where(block == 0, in_head < ROPE_DIM,
                     jnp.where(block == 1, in_head < half, (in_head >= half) & (in_head < ROPE_DIM)))
    one_piece = jnp.where((jnp.arange(2 * half)[:, None] == src_row[None, :]) & used[None, :], sign[None, :], 0.0)
    return jnp.concatenate([one_piece, one_piece], axis=0).astype(BF16)


def _inproj(x2, pos_rows, sc, sh, gain, inv_col, w, mu, *, seq, tm):
    t_rows, d = x2.shape
    tiles_per_seq = seq // tm
    aw = len(ATTN_GROUPS) * GROUP_WIDTH
    shift_w = mu.shape[1]
    assert w.shape[1] == 3 * aw + shift_w + 2 * d
    lora_w = shift_w - 3 * d
    row = lambda i: (i, 0)
    per_batch = lambda i: (i // tiles_per_seq, 0, 0)
    dils = [dil for _, dil in ATTN_GROUPS]
    out_shape = (
        [jax.ShapeDtypeStruct((t_rows // dil, dil * aw), BF16) for dil in dils]
        + [jax.ShapeDtypeStruct((t_rows, d), BF16)] * 3
        + [jax.ShapeDtypeStruct((t_rows, lora_w), F32), jax.ShapeDtypeStruct((t_rows, 2 * d), BF16)]
    )
    out_specs = (
        [pl.BlockSpec((tm // dil, dil * aw), row) for dil in dils]
        + [pl.BlockSpec((tm, d), row)] * 3
        + [pl.BlockSpec((tm, lora_w), row), pl.BlockSpec((tm, 2 * d), row)]
    )
    n_slabs = sum(3 * GROUP_WIDTH // LANES for dil in dils if dil > 1)
    spread = _rope_spread()
    return pl.pallas_call(
        functools.partial(_inproj_kernel, tiles_per_seq=tiles_per_seq),
        out_shape=out_shape,
        grid=(t_rows // tm,),
        in_specs=[
            pl.BlockSpec((tm, d), row),
            pl.BlockSpec((None, 1, tm), lambda i: (i, 0, 0)),
            pl.BlockSpec((None, 1, d), per_batch),
            pl.BlockSpec((None, 1, d), per_batch),
            _const_spec((1, d)),
            _const_spec(inv_col.shape),
            _const_spec(spread.shape),
            _const_spec(w.shape),
            _const_spec((1, shift_w)),
        ],
        out_specs=out_specs,
        scratch_shapes=[pltpu.VMEM((1, shift_w), F32), pltpu.VMEM((n_slabs, tm, LANES), F32)],
        compiler_params=pltpu.CompilerParams(
            dimension_semantics=("arbitrary",), vmem_limit_bytes=VMEM_LIMIT),
        name="inproj",
    )(x2, pos_rows, sc, sh, gain, inv_col, spread, w, mu)


def _attn_kernel(*refs, steps_per_subseq):
    n_groups = len(steps_per_subseq)
    in_refs, out_refs = refs[:2 * n_groups], refs[2 * n_groups:]
    m = pl.program_id(1)
    blk = ATTN_BLOCK
    gw = GROUP_WIDTH
    nh = ATTN_HEADS_PER_GROUP
    row = lax.broadcasted_iota(jnp.int32, (nh * blk, blk), 0) % blk
    colk = lax.broadcasted_iota(jnp.int32, (nh * blk, blk), 1)
    mask_cur = colk <= row
    mask_prev_band = colk >= row
    head_of_lane = lax.broadcasted_iota(jnp.int32, (1, gw), 1) // HEAD_DIM

    probs = []
    for g in range(n_groups):
        cur, prev = in_refs[2 * g], in_refs[2 * g + 1]
        first = mask_prev_band & ((m % steps_per_subseq[g]) > 0)
        probs.append((g, cur, slice(0, blk), prev, slice(0, blk), first))
        for nb in range(1, ATTN_BLOCKS_PER_STEP):
            probs.append((g, cur, slice(nb * blk, (nb + 1) * blk), cur, slice((nb - 1) * blk, nb * blk),
                          mask_prev_band))
    idx = range(len(probs))
    q4 = []
    for _, cur, rows, _, _, _ in probs:
        q = cur[rows, 0:gw]
        zq = jnp.zeros_like(q)
        q4.append(jnp.concatenate([jnp.where(head_of_lane == hd, q, zq) for hd in range(nh)], axis=0))
    sc = [jnp.where(mask_cur, _dot_nt(q4[i], p[1][p[2], gw:2 * gw]), NEG_INF) for i, p in zip(idx, probs)]
    sp = [jnp.where(p[5], _dot_nt(q4[i], p[3][p[4], gw:2 * gw]), NEG_INF) for i, p in zip(idx, probs)]
    mx = [jnp.max(jnp.maximum(sc[i], sp[i]), axis=-1, keepdims=True) for i in idx]
    pc = [jnp.exp(sc[i] - mx[i]) for i in idx]
    pp = [jnp.exp(sp[i] - mx[i]) for i in idx]
    den = [jnp.sum(pc[i] + pp[i], axis=-1, keepdims=True) for i in idx]
    pcb = [pc[i].astype(BF16) for i in idx]
    ppb = [pp[i].astype(BF16) for i in idx]
    low_half = lax.broadcasted_iota(jnp.int32, (1, LANES), 1) < HEAD_DIM
    pv = []
    for i, p in zip(idx, probs):
        per_pair = []
        for j in range(nh // 2):
            rows2 = slice(2 * j * blk, (2 * j + 2) * blk)
            vcol = slice(2 * gw + j * LANES, 2 * gw + (j + 1) * LANES)
            per_pair.append(_dot(pcb[i][rows2], p[1][p[2], vcol]) + _dot(ppb[i][rows2], p[3][p[4], vcol]))
        pv.append(per_pair)
    for i, (g, _, rows, _, _, _) in zip(idx, probs):
        inv_den = 1.0 / den[i]
        l4 = mx[i] + jnp.log(den[i])
        o_pairs, l_pairs = [], []
        for j in range(nh // 2):
            lo_rows = slice(2 * j * blk, (2 * j + 1) * blk)
            hi_rows = slice((2 * j + 1) * blk, (2 * j + 2) * blk)
            scaled = pv[i][j] * inv_den[2 * j * blk:(2 * j + 2) * blk]
            o_pairs.append(jnp.where(low_half, scaled[0:blk], scaled[blk:2 * blk]))
            l_pairs.append(jnp.where(low_half, l4[lo_rows], l4[hi_rows]))
        out_refs[2 * g][rows, :] = jnp.concatenate(o_pairs, axis=1).astype(BF16)
        out_refs[2 * g + 1][rows, :] = jnp.concatenate(l_pairs, axis=1)


def _attn(a_list, *, batch, seq):
    assert all(w // dil == ATTN_BLOCK for w, dil in ATTN_GROUPS)
    blk, gw = ATTN_BLOCK, GROUP_WIDTH
    per_step = ATTN_BLOCKS_PER_STEP
    n_steps = seq // (per_step * blk)
    ins, in_specs, out_shape, out_specs, nss = [], [], [], [], []
    for a, (_, dil) in zip(a_list, ATTN_GROUPS):
        assert seq % (dil * per_step * blk) == 0
        ns = n_steps // dil
        nss.append(ns)
        view = a.reshape(batch, seq // dil, dil * 3 * gw)
        cur = lambda b, m, ns=ns: (b, m % ns, m // ns)
        prev = lambda b, m, ns=ns: (b, jnp.maximum(per_step * (m % ns) - 1, 0), m // ns)
        ins += [view, view]
        in_specs += [pl.BlockSpec((None, per_step * blk, 3 * gw), cur), pl.BlockSpec((None, blk, 3 * gw), prev)]
        out_shape += [jax.ShapeDtypeStruct((batch, seq // dil, dil * gw), BF16),
                      jax.ShapeDtypeStruct((batch, seq // dil, dil * gw), F32)]
        out_specs += [pl.BlockSpec((None, per_step * blk, gw), cur)] * 2
    outs = pl.pallas_call(
        functools.partial(_attn_kernel, steps_per_subseq=tuple(nss)),
        out_shape=out_shape,
        grid=(batch, n_steps),
        in_specs=in_specs,
        out_specs=out_specs,
        compiler_params=pltpu.CompilerParams(dimension_semantics=("arbitrary", "arbitrary")),
        name="attn",
    )(*ins)
    return [o.reshape(-1, o.shape[-1]) for o in outs]


def _seg_sum(z, seg_ones):
    rows, c = z.shape
    n = c // LANES
    zs = jnp.concatenate([z[:, j * LANES:(j + 1) * LANES] for j in range(n)], axis=0).astype(BF16)
    out = _dot(zs, seg_ones)
    return jnp.concatenate([out[j * rows:(j + 1) * rows] for j in range(n)], axis=1)


def _wkv_kernel(rr_ref, rk_ref, rv_ref, rl_ref, w0_ref, w2_ref, a0_ref, a2_ref, g2_ref, kk_ref, ka_ref,
                rkp_ref, gnw_ref, gnb_ref, o_ref,
                state_ref, at_ref, rt_ref, bt_ref, kt_ref, v_ref, gam_ref, bonus_ref, g_ref):
    tt, c = rr_ref.shape
    n_pairs = c // LANES
    ck = WKV_CHUNK
    n_chunks = tt // ck
    staged = (at_ref, rt_ref, bt_ref, kt_ref, v_ref, gam_ref, bonus_ref, g_ref)

    @pl.when(pl.program_id(1) == 0)
    def _():
        state_ref[...] = jnp.zeros_like(state_ref)
        for ref in staged:
            ref[...] = jnp.zeros_like(ref)

    at_n, rt_n, bt_n, kt_n, v_n, gam_n, bonus_p, g_p = [ref[...] for ref in staged]

    lane = lax.broadcasted_iota(jnp.int32, (LANES, LANES), 1)
    rowi = lax.broadcasted_iota(jnp.int32, (LANES, LANES), 0)
    seg_ones = jnp.where(lane // HEAD_DIM == rowi // HEAD_DIM, 1.0, 0.0).astype(BF16)

    def stage_lora():
        lora = rl_ref[...]
        dwa = lora[:, 0:LANES]
        dg = lora[:, LANES:2 * LANES]
        zw = w0_ref[...] + _dot(jnp.tanh(dwa).astype(BF16), w2_ref[...])
        za = a0_ref[...] + _dot(dwa.astype(BF16), a2_ref[...])
        return zw, za, _dot(_sigmoid(dg).astype(BF16), g2_ref[...])

    tri2 = (lax.broadcasted_iota(jnp.int32, (ck, 2 * ck), 1) % ck
            <= lax.broadcasted_iota(jnp.int32, (ck, 2 * ck), 0)).astype(BF16)

    def stage_chunk(ci, zw, za):
        rows = slice(ci * ck, (ci + 1) * ck)
        r = rr_ref[rows, :].astype(F32)
        k = rk_ref[rows, :].astype(F32)
        v = rv_ref[rows, :]
        zw = zw[rows]
        w_log = -(jnp.maximum(-zw, 0.0) + jnp.log(1.0 + jnp.exp(-jnp.abs(zw)))) - 0.5
        lw = -jnp.exp(w_log)
        a = _sigmoid(za[rows])
        kk = k * kk_ref[...]
        kk = kk * lax.rsqrt(jnp.maximum(_seg_sum(kk * kk, seg_ones), 1e-24))
        kmod = k * (1.0 + (a - 1.0) * ka_ref[...])
        bonus = _seg_sum(r * kmod * rkp_ref[...], seg_ones) * v.astype(F32)
        hi, lo = _split2(lw)
        cum = _dot(tri2, jnp.concatenate([hi, lo], axis=0))
        e_in = jnp.exp(cum)
        e_neg = jnp.exp(-cum)
        return (
            (-kk * jnp.exp(cum - lw)).astype(BF16),
            (r * e_in).astype(BF16),
            (kk * a * e_neg).astype(BF16),
            (kmod * e_neg).astype(BF16),
            v,
            e_in[ck - 8:ck],
            bonus,
        )

    t_idx = lax.broadcasted_iota(jnp.int32, (ck, LANES), 0)
    i_idx = lax.broadcasted_iota(jnp.int32, (ck, LANES), 1) % ck
    strict = i_idx < t_idx
    incl = i_idx <= t_idx
    lo_mask = lax.broadcasted_iota(jnp.int32, (1, LANES), 1) < HEAD_DIM
    eye_cat = jnp.where(i_idx == t_idx, 1.0, 0.0)
    bk_row_head = lax.broadcasted_iota(jnp.int32, (LANES, 2 * LANES), 0) // HEAD_DIM
    bk_col_head = (lax.broadcasted_iota(jnp.int32, (LANES, 2 * LANES), 1) // ck) % 2
    bk_mask = bk_row_head == bk_col_head

    def entries(t):
        return jnp.stack([t[ci * ck:(ci + 1) * ck, p * LANES:(p + 1) * LANES]
                          for ci in range(n_chunks) for p in range(n_pairs)])

    def stack_heads(t):
        z = jnp.zeros_like(t)
        return jnp.concatenate([jnp.where(lo_mask, t, z), jnp.where(lo_mask, z, t)], axis=1).astype(BF16)

    def bmm(a_, b_):
        return jnp.einsum('pmk,pkn->pmn', a_, b_, preferred_element_type=F32)

    at = entries(at_n)
    rt = entries(rt_n)
    bt = entries(bt_n)
    kt = entries(kt_n)
    bk_t = jnp.swapaxes(jnp.concatenate([bt, bt, kt, kt], axis=1), 1, 2)
    bk_t = jnp.where(bk_mask, bk_t, jnp.zeros_like(bk_t))
    vw = stack_heads(entries(v_n))
    gcol = jnp.swapaxes(
        jnp.stack([gam_n[ci * 8:(ci + 1) * 8, p * LANES:(p + 1) * LANES]
                   for ci in range(n_chunks) for p in range(n_pairs)]), 1, 2)[:, :, 7:8]

    gm = bmm(jnp.concatenate([at, rt], axis=1), bk_t)
    a_ab = jnp.where(strict, gm[:, :ck, :LANES], 0.0)
    a_ak = jnp.where(strict, gm[:, :ck, LANES:], 0.0).astype(BF16)
    a_rb = jnp.where(incl, gm[:, ck:, :LANES], 0.0).astype(BF16)
    a_rk = jnp.where(incl, gm[:, ck:, LANES:], 0.0).astype(BF16)
    zw_t, za_t, g_new = stage_lora()
    new_chunks = []
    stage_order = iter(range(n_chunks))

    def stage_next():
        ci = next(stage_order, None)
        if ci is not None:
            new_chunks.append(stage_chunk(ci, zw_t, za_t))

    stage_next()
    tm_ = eye_cat + a_ab
    pw = bmm(a_ab.astype(BF16), stack_heads(a_ab))
    for step in range(4):
        both = bmm(jnp.concatenate([tm_, pw], axis=1).astype(BF16), stack_heads(pw))
        tm_ = tm_ + both[:, :ck, :]
        pw = both[:, ck:, :]
        if step % 2 == 0:
            stage_next()
    tm_ = tm_ + bmm(tm_.astype(BF16), stack_heads(pw))
    t_cat = tm_.astype(BF16)
    by_v = bmm(jnp.concatenate([a_ak, a_rk, bk_t[:, :, LANES:]], axis=1), vw)
    ta = bmm(t_cat, stack_heads(at))
    for _ in range(n_chunks):
        stage_next()
    tx = bmm(t_cat, stack_heads(by_v[:, :ck, :]))
    yk = by_v[:, ck:2 * ck, :]
    dk = by_v[:, 2 * ck:, :]
    new_staged = [jnp.concatenate(parts, axis=0) for parts in zip(*new_chunks)] + [g_new]
    tar = jnp.concatenate([ta.astype(BF16), rt], axis=1)
    ba = jnp.concatenate([bk_t[:, :, :LANES], a_rb], axis=1)

    s = state_ref[...]
    y_rows = []
    for ci in range(n_chunks):
        es = slice(ci * n_pairs, (ci + 1) * n_pairs)
        o1 = bmm(tar[es], s.astype(BF16))
        u = o1[:, :ck, :] + tx[es]
        o2 = bmm(ba[es], stack_heads(u))
        y = o1[:, ck:, :] + o2[:, LANES:, :] + yk[es]
        y_rows.append(jnp.concatenate([y[p] for p in range(n_pairs)], axis=1))
        s = (s + o2[:, :LANES, :] + dk[es]) * gcol[es]
    state_ref[...] = s

    y = jnp.concatenate(y_rows, axis=0)
    inv_n = 1.0 / HEAD_DIM
    mean = _seg_sum(y, seg_ones) * inv_n
    ycen = y - mean
    var = _seg_sum(ycen * ycen, seg_ones) * inv_n
    yn = ycen * lax.rsqrt(var + GN_EPS) * gnw_ref[...] + gnb_ref[...]
    o_ref[...] = ((yn + bonus_p) * g_p).astype(o_ref.dtype)

    for ref, val in zip(staged, new_staged):
        ref[...] = val


def _wkv(rr, rk, rv, rl, w0, w2p, a0, a2p, g2, k_k, k_a, r_k, gn_w, gn_b, *, batch, seq, tt):
    t_rows, c = rr.shape
    steps = seq // tt
    row_in = lambda b, j: (b * steps + jnp.minimum(j, steps - 1), 0)
    row_out = lambda b, j: (b * steps + jnp.maximum(j - 1, 0), 0)
    vec = _const_spec((1, c))
    n_gam = (tt // WKV_CHUNK) * 8
    return pl.pallas_call(
        _wkv_kernel,
        out_shape=jax.ShapeDtypeStruct((t_rows, c), BF16),
        grid=(batch, steps + 1),
        in_specs=[
            pl.BlockSpec((tt, c), row_in), pl.BlockSpec((tt, c), row_in), pl.BlockSpec((tt, c), row_in),
            pl.BlockSpec((tt, rl.shape[1]), row_in),
            vec, _const_spec(w2p.shape), vec, _const_spec(a2p.shape), _const_spec(g2.shape),
            vec, vec, vec, vec, vec,
        ],
        out_specs=pl.BlockSpec((tt, c), row_out),
        scratch_shapes=[pltpu.VMEM((c // LANES, LANES, LANES), F32)]
        + [pltpu.VMEM((tt, c), BF16)] * 5
        + [pltpu.VMEM((n_gam, c), F32), pltpu.VMEM((tt, c), F32), pltpu.VMEM((tt, c), F32)],
        compiler_params=pltpu.CompilerParams(
            dimension_semantics=("arbitrary", "arbitrary"), vmem_limit_bytes=VMEM_LIMIT),
        name="wkv",
    )(rr, rk, rv, rl, w0, w2p, a0, a2p, g2, k_k, k_a, r_k, gn_w, gn_b)


def _branch_kernel(o0_ref, l0_ref, o1_ref, l1_ref, o2_ref, l2_ref, gate_ref, orw_ref, x_ref,
                   wba_ref, wbr_ref, wo_ref, gt_ref, sc_ref, sh_ref, gpost_ref, gpre_ref,
                   x1_ref, h2_ref, nat_ref):
    tm, d = x_ref.shape
    gw = GROUP_WIDTH

    def natural(ref, dil, slab):
        if dil == 1:
            return ref[...].astype(F32)
        for r in range(dil):
            for s in range(gw // LANES):
                nat_ref[slab + s, pl.ds(r, tm // dil, stride=dil), :] = (
                    ref[:, r * gw + s * LANES:r * gw + (s + 1) * LANES].astype(F32))
        return jnp.concatenate([nat_ref[slab + s] for s in range(gw // LANES)], axis=1)

    dils = [dil for _, dil in ATTN_GROUPS]
    per_group = gw // LANES
    o0, o1, o2 = [natural(ref, dil, 2 * g * per_group)
                  for g, (ref, dil) in enumerate(zip((o0_ref, o1_ref, o2_ref), dils))]
    l0, l1, l2 = [natural(ref, dil, (2 * g + 1) * per_group)
                  for g, (ref, dil) in enumerate(zip((l0_ref, l1_ref, l2_ref), dils))]
    mx = jnp.maximum(jnp.maximum(l0, l1), l2)
    e0, e1, e2 = jnp.exp(l0 - mx), jnp.exp(l1 - mx), jnp.exp(l2 - mx)
    o_attn = (e0 * o0 + e1 * o1 + e2 * o2) / (e0 + e1 + e2)
    ya = _dot(o_attn.astype(BF16), wba_ref[...].astype(BF16))
    yr = _dot(orw_ref[...], wbr_ref[...].astype(BF16))
    merged = gate_ref[:, 0:d].astype(F32) * ya + gate_ref[:, d:2 * d].astype(F32) * yr
    mix = _dot(merged.astype(BF16), wo_ref[...].astype(BF16))
    x1 = x_ref[...] + gt_ref[...] * (_rms(mix) * gpost_ref[...])
    x1_ref[...] = x1
    h2 = _rms(x1) * gpre_ref[...]
    h2_ref[...] = (h2 * (1.0 + sc_ref[...]) + sh_ref[...]).astype(BF16)


def _branch(attn_outs, gates, o_rwkv, x2, wba, wbr, wo, gt, sc, sh, gpost, gpre, *, seq, tm):
    t_rows, d = x2.shape
    tiles_per_seq = seq // tm
    row = lambda i: (i, 0)
    per_batch = lambda i: (i // tiles_per_seq, 0, 0)
    gw = GROUP_WIDTH
    attn_specs = [pl.BlockSpec((tm // dil, dil * gw), row) for _, dil in ATTN_GROUPS for _ in range(2)]
    n_slabs = 2 * len(ATTN_GROUPS) * (gw // LANES)
    return pl.pallas_call(
        _branch_kernel,
        out_shape=[jax.ShapeDtypeStruct((t_rows, d), F32), jax.ShapeDtypeStruct((t_rows, d), BF16)],
        grid=(t_rows // tm,),
        scratch_shapes=[pltpu.VMEM((n_slabs, tm, LANES), F32)],
        in_specs=attn_specs + [
            pl.BlockSpec((tm, 2 * d), row), pl.BlockSpec((tm, d), row), pl.BlockSpec((tm, d), row),
            _const_spec(wba.shape), _const_spec(wbr.shape), _const_spec(wo.shape),
            pl.BlockSpec((None, 1, d), per_batch), pl.BlockSpec((None, 1, d), per_batch),
            pl.BlockSpec((None, 1, d), per_batch),
            _const_spec((1, d)), _const_spec((1, d)),
        ],
        out_specs=[pl.BlockSpec((tm, d), row), pl.BlockSpec((tm, d), row)],
        compiler_params=pltpu.CompilerParams(
            dimension_semantics=("arbitrary",), vmem_limit_bytes=VMEM_LIMIT),
        name="branch",
    )(*attn_outs, gates, o_rwkv, x2, wba, wbr, wo, gt, sc, sh, gpost, gpre)


def _ffn_kernel(h_ref, x_ref, w1_ref, w2_ref, gt_ref, gpost_ref, o_ref, *, tf):
    h = h_ref[...]
    acc = jnp.zeros(x_ref.shape, F32)
    for j in range(w1_ref.shape[1] // tf):
        a = jnp.maximum(_dot(h, w1_ref[:, j * tf:(j + 1) * tf].astype(BF16)), 0.0)
        acc = acc + _dot((a * a).astype(BF16), w2_ref[j * tf:(j + 1) * tf, :].astype(BF16))
    o_ref[...] = x_ref[...] + gt_ref[...] * (_rms(acc) * gpost_ref[...])


def _ffn(h2, x1, w1, w2, gt, gpost, *, seq, tm, tf):
    t_rows, d = x1.shape
    tiles_per_seq = seq // tm
    row = lambda i: (i, 0)
    per_batch = lambda i: (i // tiles_per_seq, 0, 0)
    return pl.pallas_call(
        functools.partial(_ffn_kernel, tf=tf),
        out_shape=jax.ShapeDtypeStruct((t_rows, d), F32),
        grid=(t_rows // tm,),
        in_specs=[
            pl.BlockSpec((tm, d), row), pl.BlockSpec((tm, d), row),
            _const_spec(w1.shape), _const_spec(w2.shape),
            pl.BlockSpec((None, 1, d), per_batch), _const_spec((1, d)),
        ],
        out_specs=pl.BlockSpec((tm, d), row),
        compiler_params=pltpu.CompilerParams(
            dimension_semantics=("arbitrary",), vmem_limit_bytes=VMEM_LIMIT),
        name="ffn",
    )(h2, x1, w1, w2, gt, gpost)


def kernel(x, c, positions, ada_w, ada_b, norm_mix_pre, norm_mix_post, norm_ffn_pre, norm_ffn_post, w_in, shift_mu, decay_w0, decay_w2, iclr_a0, iclr_a2, gate_g2, k_k, k_a, r_k, gn_w, gn_b, w_branch, w_out, w_ff1, w_ff2):
    batch, seq, d = x.shape
    depth = ada_w.shape[0]
    t_rows = batch * seq
    n_groups = len(ATTN_GROUPS)
    attn_out_w = GROUP_WIDTH
    shift_w = shift_mu.shape[1]
    assert shift_w == 3 * d + DECAY_LORA + ICLR_LORA + GATE_LORA
    assert DECAY_LORA + ICLR_LORA == LANES and GATE_LORA == LANES
    tm = 512

    x2 = x.reshape(t_rows, d)
    pos_rows = positions.astype(F32).reshape(t_rows // tm, 1, tm)
    half = ROPE_DIM // 2
    inv_col = (ROPE_THETA ** (-jnp.arange(half, dtype=F32) * 2.0 / ROPE_DIM)).reshape(half, 1)
    c_pad = jnp.pad(c, ((0, 8 - batch), (0, 0)))
    vec = lambda p: p.reshape(1, -1)

    for l in range(depth):
        mod = _mod(c_pad, ada_w[l], ada_b[l].reshape(1, -1))[:batch]
        sh1, sc1, gt1, sh2, sc2, gt2 = [m.reshape(batch, 1, d) for m in jnp.split(mod, 6, axis=-1)]

        a0, a1, a2, rr, rk, rv, rl, gates = _inproj(
            x2, pos_rows, sc1, sh1, vec(norm_mix_pre[l]), inv_col, w_in[l].astype(BF16), vec(shift_mu[l]),
            seq=seq, tm=tm)

        attn_outs = _attn([a0, a1, a2], batch=batch, seq=seq)

        zeros = jnp.zeros((DECAY_LORA, d), BF16)
        w2p = jnp.concatenate([decay_w2[l].astype(BF16), zeros], axis=0)
        a2p = jnp.concatenate([zeros, iclr_a2[l].astype(BF16)], axis=0)
        o_rwkv = _wkv(rr, rk, rv, rl, vec(decay_w0[l]), w2p, vec(iclr_a0[l]), a2p,
                      gate_g2[l].astype(BF16), vec(k_k[l]), vec(k_a[l]), vec(r_k[l]),
                      vec(gn_w[l]), vec(gn_b[l]), batch=batch, seq=seq, tt=256)

        wb = w_branch[l]
        x1, h2 = _branch(attn_outs, gates, o_rwkv, x2, wb[:attn_out_w], wb[attn_out_w:],
                         w_out[l], gt1, sc2, sh2, vec(norm_mix_post[l]),
                         vec(norm_ffn_pre[l]), seq=seq, tm=tm)
        x2 = _ffn(h2, x1, w_ff1[l], w_ff2[l], gt2, vec(norm_ffn_post[l]),
                  seq=seq, tm=tm, tf=512)
    return x2.reshape(batch, seq, d)
```

```python
import functools

import jax
import jax.numpy as jnp
from jax import lax
from jax.experimental import pallas as pl
from jax.experimental.pallas import tpu as pltpu

F32 = jnp.float32
BF16 = jnp.bfloat16

HEAD_DIM = 64
ATTN_GROUPS = ((128, 1), (512, 4), (2048, 16))
ATTN_HEADS_PER_GROUP = 4
ATTN_BLOCK = 128
ATTN_BLOCKS_PER_STEP = 4
GROUP_WIDTH = ATTN_HEADS_PER_GROUP * HEAD_DIM
ROPE_THETA = 500000.0
ROPE_DIM = HEAD_DIM // 4
DECAY_LORA = 64
ICLR_LORA = 64
GATE_LORA = 128
NORM_EPS = 1e-6
GN_EPS = 64e-5
NEG_INF = -1e30

LANES = 128
VMEM_LIMIT = 56 * 1024 * 1024
TOKEN_TILE = 512
FFN_CHUNK = 512
WKV_CHUNK = 64
WKV_TILE = 256


def _dot(a, b):
    return jnp.dot(a, b, preferred_element_type=F32)


def _dot_nt(a, b):
    return lax.dot_general(a, b, (((1,), (1,)), ((), ())), preferred_element_type=F32)


def _split2(x):
    hi = x.astype(BF16)
    lo = (x - hi.astype(F32)).astype(BF16)
    return hi, lo


def _sigmoid(x):
    return 1.0 / (1.0 + jnp.exp(-x))


def _rms(t):
    return t * lax.rsqrt(jnp.mean(t * t, axis=-1, keepdims=True) + NORM_EPS)


def _const_spec(shape):
    return pl.BlockSpec(shape, lambda *_: (0,) * len(shape), pipeline_mode=pl.Buffered(1))


def _mod_kernel(c_ref, w_ref, b_ref, win_ref, o_ref, winb_ref):
    c = c_ref[...]
    a = c * _sigmoid(c)
    ah, al = _split2(a)
    wh, wl = _split2(w_ref[...])
    o_ref[...] = _dot(ah, wh) + _dot(ah, wl) + _dot(al, wh) + b_ref[...]
    winb_ref[...] = win_ref[...].astype(BF16)


def _mod(c_pad, ada_w, ada_b, w_in):
    rows, d = c_pad.shape
    n = ada_w.shape[1]
    n_in = w_in.shape[1]
    steps = 6
    tn, tn_in = n // steps, n_in // steps
    assert n % steps == 0 and n_in % steps == 0 and tn % LANES == 0 and tn_in % LANES == 0
    col = lambda j: (0, j)
    return pl.pallas_call(
        _mod_kernel,
        out_shape=[jax.ShapeDtypeStruct((rows, n), F32), jax.ShapeDtypeStruct(w_in.shape, BF16)],
        grid=(steps,),
        in_specs=[
            pl.BlockSpec((rows, d), lambda j: (0, 0)),
            pl.BlockSpec((d, tn), col),
            pl.BlockSpec((1, tn), col),
            pl.BlockSpec((d, tn_in), col),
        ],
        out_specs=[pl.BlockSpec((rows, tn), col), pl.BlockSpec((d, tn_in), col)],
        name="mod",
    )(c_pad, ada_w, ada_b, w_in)


def _inproj_kernel(x_ref, pos_ref, sc_ref, sh_ref, g_ref, inv_ref, spread_ref, w_ref, mu_ref,
                   a0_ref, a1_ref, a2_ref, rr_ref, rk_ref, rv_ref, rl_ref, gate_ref, carry_ref, dil_ref,
                   *, tiles_per_seq):
    i = pl.program_id(0)
    tm = x_ref.shape[0]
    d = x_ref.shape[1]
    h = _rms(x_ref[...]) * g_ref[...]
    hb = (h * (1.0 + sc_ref[...]) + sh_ref[...]).astype(BF16)

    half = ROPE_DIM // 2
    ang = inv_ref[...] * pos_ref[...]
    cs_t = jnp.concatenate([jnp.cos(ang), jnp.sin(ang)], axis=0).T
    tab = _dot(jnp.concatenate(_split2(cs_t), axis=1), spread_ref[...])
    lane = lax.broadcasted_iota(jnp.int32, (1, LANES), 1) % HEAD_DIM
    cc = tab[:, 0:LANES] + jnp.where(lane < ROPE_DIM, 0.0, 1.0)
    s_lo = tab[:, LANES:2 * LANES]
    s_hi = tab[:, 2 * LANES:3 * LANES]

    def rope(t):
        return t * cc + pltpu.roll(t, LANES - half, 1) * s_lo + pltpu.roll(t, half, 1) * s_hi

    a_refs = (a0_ref, a1_ref, a2_ref)
    aw = len(ATTN_GROUPS) * GROUP_WIDTH
    gw = GROUP_WIDTH
    slab = 0
    for which in range(3):
        t = _dot(hb, w_ref[:, which * aw:(which + 1) * aw])
        for g, (_, dil) in enumerate(ATTN_GROUPS):
            halves = [t[:, g * gw + s * LANES:g * gw + (s + 1) * LANES] for s in range(gw // LANES)]
            if which < 2:
                halves = [rope(hv) for hv in halves]
            if which == 0:
                halves = [hv * (HEAD_DIM ** -0.5) for hv in halves]
            for s, hv in enumerate(halves):
                if dil == 1:
                    a_refs[g][:, which * gw + s * LANES:which * gw + (s + 1) * LANES] = hv.astype(BF16)
                    continue
                dil_ref[slab] = hv
                for r in range(dil):
                    col = r * 3 * gw + which * gw + s * LANES
                    a_refs[g][:, col:col + LANES] = dil_ref[slab, pl.ds(r, tm // dil, stride=dil), :].astype(BF16)
                slab += 1

    first = (i % tiles_per_seq) == 0
    row0 = lax.broadcasted_iota(jnp.int32, (tm, 1), 0) == 0
    shift_col = 3 * aw
    gate_col = shift_col + mu_ref.shape[1]
    col = 0
    for ref in (rr_ref, rk_ref, rv_ref, rl_ref):
        n = ref.shape[1]
        u = _dot(hb, w_ref[:, shift_col + col:shift_col + col + n])
        carry = jnp.where(first, 0.0, carry_ref[:, col:col + n])
        prev = jnp.where(row0, carry, pltpu.roll(u, 1, 0))
        carry_ref[:, col:col + n] = u[tm - 1:tm, :]
        ref[...] = (u + (prev - u) * mu_ref[:, col:col + n]).astype(ref.dtype)
        col += n

    for j in range(2):
        gj = _dot(hb, w_ref[:, gate_col + j * d:gate_col + (j + 1) * d])
        gate_ref[:, j * d:(j + 1) * d] = _sigmoid(gj).astype(BF16)


def _rope_spread():
    half = ROPE_DIM // 2
    cols = jnp.arange(3 * LANES)
    block, in_head = cols // LANES, cols % HEAD_DIM
    freq = in_head % half
    src_row = jnp.where(block == 0, freq, half + freq)
    sign = jnp.where(block == 1, -1.0, 1.0)
    used = jnp.where(block == 0, in_head < ROPE_DIM,
                     jnp.where(block == 1, in_head < half, (in_head >= half) & (in_head < ROPE_DIM)))
    one_piece = jnp.where((jnp.arange(2 * half)[:, None] == src_row[None, :]) & used[None, :], sign[None, :], 0.0)
    return jnp.concatenate([one_piece, one_piece], axis=0).astype(BF16)


def _inproj(x2, pos_rows, sc, sh, gain, inv_col, w, mu, *, seq, tm):
    t_rows, d = x2.shape
    tiles_per_seq = seq // tm
    aw = len(ATTN_GROUPS) * GROUP_WIDTH
    shift_w = mu.shape[1]
    assert w.shape[1] == 3 * aw + shift_w + 2 * d
    lora_w = shift_w - 3 * d
    row = lambda i: (i, 0)
    per_batch = lambda i: (i // tiles_per_seq, 0, 0)
    dils = [dil for _, dil in ATTN_GROUPS]
    out_shape = (
        [jax.ShapeDtypeStruct((t_rows // dil, dil * aw), BF16) for dil in dils]
        + [jax.ShapeDtypeStruct((t_rows, d), BF16)] * 3
        + [jax.ShapeDtypeStruct((t_rows, lora_w), F32), jax.ShapeDtypeStruct((t_rows, 2 * d), BF16)]
    )
    out_specs = (
        [pl.BlockSpec((tm // dil, dil * aw), row) for dil in dils]
        + [pl.BlockSpec((tm, d), row)] * 3
        + [pl.BlockSpec((tm, lora_w), row), pl.BlockSpec((tm, 2 * d), row)]
    )
    n_slabs = sum(3 * GROUP_WIDTH // LANES for dil in dils if dil > 1)
    spread = _rope_spread()
    return pl.pallas_call(
        functools.partial(_inproj_kernel, tiles_per_seq=tiles_per_seq),
        out_shape=out_shape,
        grid=(t_rows // tm,),
        in_specs=[
            pl.BlockSpec((tm, d), row),
            pl.BlockSpec((None, 1, tm), lambda i: (i, 0, 0)),
            pl.BlockSpec((None, 1, d), per_batch),
            pl.BlockSpec((None, 1, d), per_batch),
            _const_spec((1, d)),
            _const_spec(inv_col.shape),
            _const_spec(spread.shape),
            _const_spec(w.shape),
            _const_spec((1, shift_w)),
        ],
        out_specs=out_specs,
        scratch_shapes=[pltpu.VMEM((1, shift_w), F32), pltpu.VMEM((n_slabs, tm, LANES), F32)],
        compiler_params=pltpu.CompilerParams(
            dimension_semantics=("arbitrary",), vmem_limit_bytes=VMEM_LIMIT),
        name="inproj",
    )(x2, pos_rows, sc, sh, gain, inv_col, spread, w, mu)


def _attn_kernel(*refs, steps_per_subseq):
    n_groups = len(steps_per_subseq)
    in_refs, out_refs = refs[:2 * n_groups], refs[2 * n_groups:]
    m = pl.program_id(1)
    blk = ATTN_BLOCK
    gw = GROUP_WIDTH
    nh = ATTN_HEADS_PER_GROUP
    row = lax.broadcasted_iota(jnp.int32, (nh * blk, blk), 0) % blk
    colk = lax.broadcasted_iota(jnp.int32, (nh * blk, blk), 1)
    mask_cur = colk <= row
    mask_prev_band = colk >= row
    head_of_lane = lax.broadcasted_iota(jnp.int32, (1, gw), 1) // HEAD_DIM

    probs = []
    for g in range(n_groups):
        cur, prev = in_refs[2 * g], in_refs[2 * g + 1]
        first = mask_prev_band & ((m % steps_per_subseq[g]) > 0)
        probs.append((g, cur, slice(0, blk), prev, slice(0, blk), first))
        for nb in range(1, ATTN_BLOCKS_PER_STEP):
            probs.append((g, cur, slice(nb * blk, (nb + 1) * blk), cur, slice((nb - 1) * blk, nb * blk),
                          mask_prev_band))
    idx = range(len(probs))
    q4 = []
    for _, cur, rows, _, _, _ in probs:
        q = cur[rows, 0:gw]
        zq = jnp.zeros_like(q)
        q4.append(jnp.concatenate([jnp.where(head_of_lane == hd, q, zq) for hd in range(nh)], axis=0))
    sc = [jnp.where(mask_cur, _dot_nt(q4[i], p[1][p[2], gw:2 * gw]), NEG_INF) for i, p in zip(idx, probs)]
    sp = [jnp.where(p[5], _dot_nt(q4[i], p[3][p[4], gw:2 * gw]), NEG_INF) for i, p in zip(idx, probs)]
    mx = [jnp.max(jnp.maximum(sc[i], sp[i]), axis=-1, keepdims=True) for i in idx]
    pc = [jnp.exp(sc[i] - mx[i]) for i in idx]
    pp = [jnp.exp(sp[i] - mx[i]) for i in idx]
    den = [jnp.sum(pc[i] + pp[i], axis=-1, keepdims=True) for i in idx]
    pcb = [pc[i].astype(BF16) for i in idx]
    ppb = [pp[i].astype(BF16) for i in idx]
    low_half = lax.broadcasted_iota(jnp.int32, (1, LANES), 1) < HEAD_DIM
    pv = []
    for i, p in zip(idx, probs):
        per_pair = []
        for j in range(nh // 2):
            rows2 = slice(2 * j * blk, (2 * j + 2) * blk)
            vcol = slice(2 * gw + j * LANES, 2 * gw + (j + 1) * LANES)
            per_pair.append(_dot(pcb[i][rows2], p[1][p[2], vcol]) + _dot(ppb[i][rows2], p[3][p[4], vcol]))
        pv.append(per_pair)
    for i, (g, _, rows, _, _, _) in zip(idx, probs):
        inv_den = 1.0 / den[i]
        l4 = mx[i] + jnp.log(den[i])
        o_pairs, l_pairs = [], []
        for j in range(nh // 2):
            lo_rows = slice(2 * j * blk, (2 * j + 1) * blk)
            hi_rows = slice((2 * j + 1) * blk, (2 * j + 2) * blk)
            scaled = pv[i][j] * inv_den[2 * j * blk:(2 * j + 2) * blk]
            o_pairs.append(jnp.where(low_half, scaled[0:blk], scaled[blk:2 * blk]))
            l_pairs.append(jnp.where(low_half, l4[lo_rows], l4[hi_rows]))
        out_refs[2 * g][rows, :] = jnp.concatenate(o_pairs, axis=1).astype(BF16)
        out_refs[2 * g + 1][rows, :] = jnp.concatenate(l_pairs, axis=1)


def _attn(a_list, *, batch, seq):
    assert all(w // dil == ATTN_BLOCK for w, dil in ATTN_GROUPS)
    blk, gw = ATTN_BLOCK, GROUP_WIDTH
    per_step = ATTN_BLOCKS_PER_STEP
    n_steps = seq // (per_step * blk)
    ins, in_specs, out_shape, out_specs, nss = [], [], [], [], []
    for a, (_, dil) in zip(a_list, ATTN_GROUPS):
        assert seq % (dil * per_step * blk) == 0
        ns = n_steps // dil
        nss.append(ns)
        view = a.reshape(batch, seq // dil, dil * 3 * gw)
        cur = lambda b, m, ns=ns: (b, m % ns, m // ns)
        prev = lambda b, m, ns=ns: (b, jnp.maximum(per_step * (m % ns) - 1, 0), m // ns)
        ins += [view, view]
        in_specs += [pl.BlockSpec((None, per_step * blk, 3 * gw), cur), pl.BlockSpec((None, blk, 3 * gw), prev)]
        out_shape += [jax.ShapeDtypeStruct((batch, seq // dil, dil * gw), BF16),
                      jax.ShapeDtypeStruct((batch, seq // dil, dil * gw), F32)]
        out_specs += [pl.BlockSpec((None, per_step * blk, gw), cur)] * 2
    outs = pl.pallas_call(
        functools.partial(_attn_kernel, steps_per_subseq=tuple(nss)),
        out_shape=out_shape,
        grid=(batch, n_steps),
        in_specs=in_specs,
        out_specs=out_specs,
        compiler_params=pltpu.CompilerParams(dimension_semantics=("arbitrary", "arbitrary")),
        name="attn",
    )(*ins)
    return [o.reshape(-1, o.shape[-1]) for o in outs]


def _seg_sum(z, seg_ones):
    rows, c = z.shape
    n = c // LANES
    zs = jnp.concatenate([z[:, j * LANES:(j + 1) * LANES] for j in range(n)], axis=0).astype(BF16)
    out = _dot(zs, seg_ones)
    return jnp.concatenate([out[j * rows:(j + 1) * rows] for j in range(n)], axis=1)


def _wkv_kernel(rr_ref, rk_ref, rv_ref, rl_ref, w0_ref, w2_ref, a0_ref, a2_ref, g2_ref, kk_ref, ka_ref,
                rkp_ref, gnw_ref, gnb_ref, o_ref,
                state_ref, at_ref, rt_ref, bt_ref, kt_ref, v_ref, gam_ref, bonus_ref, g_ref):
    tt, c = rr_ref.shape
    n_pairs = c // LANES
    ck = WKV_CHUNK
    n_chunks = tt // ck
    staged = (at_ref, rt_ref, bt_ref, kt_ref, v_ref, gam_ref, bonus_ref, g_ref)

    @pl.when(pl.program_id(1) == 0)
    def _():
        state_ref[...] = jnp.zeros_like(state_ref)
        for ref in staged:
            ref[...] = jnp.zeros_like(ref)

    at_n, rt_n, bt_n, kt_n, v_n, gam_n, bonus_p, g_p = [ref[...] for ref in staged]

    lane = lax.broadcasted_iota(jnp.int32, (LANES, LANES), 1)
    rowi = lax.broadcasted_iota(jnp.int32, (LANES, LANES), 0)
    seg_ones = jnp.where(lane // HEAD_DIM == rowi // HEAD_DIM, 1.0, 0.0).astype(BF16)

    def stage_lora():
        lora = rl_ref[...]
        dwa = lora[:, 0:LANES]
        dg = lora[:, LANES:2 * LANES]
        zw = w0_ref[...] + _dot(jnp.tanh(dwa).astype(BF16), w2_ref[...])
        za = a0_ref[...] + _dot(dwa.astype(BF16), a2_ref[...])
        return zw, za, _dot(_sigmoid(dg).astype(BF16), g2_ref[...])

    tri2 = (lax.broadcasted_iota(jnp.int32, (ck, 2 * ck), 1) % ck
            <= lax.broadcasted_iota(jnp.int32, (ck, 2 * ck), 0)).astype(BF16)

    def stage_chunk(ci, zw, za):
        rows = slice(ci * ck, (ci + 1) * ck)
        r = rr_ref[rows, :].astype(F32)
        k = rk_ref[rows, :].astype(F32)
        v = rv_ref[rows, :]
        zw = zw[rows]
        w_log = -(jnp.maximum(-zw, 0.0) + jnp.log(1.0 + jnp.exp(-jnp.abs(zw)))) - 0.5
        lw = -jnp.exp(w_log)
        a = _sigmoid(za[rows])
        kk = k * kk_ref[...]
        kk = kk * lax.rsqrt(jnp.maximum(_seg_sum(kk * kk, seg_ones), 1e-24))
        kmod = k * (1.0 + (a - 1.0) * ka_ref[...])
        bonus = _seg_sum(r * kmod * rkp_ref[...], seg_ones) * v.astype(F32)
        hi, lo = _split2(lw)
        cum = _dot(tri2, jnp.concatenate([hi, lo], axis=0))
        e_in = jnp.exp(cum)
        e_neg = jnp.exp(-cum)
        return (
            (-kk * jnp.exp(cum - lw)).astype(BF16),
            (r * e_in).astype(BF16),
            (kk * a * e_neg).astype(BF16),
            (kmod * e_neg).astype(BF16),
            v,
            e_in[ck - 8:ck],
            bonus,
        )

    t_idx = lax.broadcasted_iota(jnp.int32, (ck, LANES), 0)
    i_idx = lax.broadcasted_iota(jnp.int32, (ck, LANES), 1) % ck
    strict = i_idx < t_idx
    incl = i_idx <= t_idx
    lo_mask = lax.broadcasted_iota(jnp.int32, (1, LANES), 1) < HEAD_DIM
    eye_cat = jnp.where(i_idx == t_idx, 1.0, 0.0)
    bk_row_head = lax.broadcasted_iota(jnp.int32, (LANES, 2 * LANES), 0) // HEAD_DIM
    bk_col_head = (lax.broadcasted_iota(jnp.int32, (LANES, 2 * LANES), 1) // ck) % 2
    bk_mask = bk_row_head == bk_col_head

    def entries(t):
        return jnp.stack([t[ci * ck:(ci + 1) * ck, p * LANES:(p + 1) * LANES]
                          for ci in range(n_chunks) for p in range(n_pairs)])

    def stack_heads(t):
        z = jnp.zeros_like(t)
        return jnp.concatenate([jnp.where(lo_mask, t, z), jnp.where(lo_mask, z, t)], axis=1).astype(BF16)

    def bmm(a_, b_):
        return jnp.einsum('pmk,pkn->pmn', a_, b_, preferred_element_type=F32)

    at = entries(at_n)
    rt = entries(rt_n)
    bt = entries(bt_n)
    kt = entries(kt_n)
    bk_t = jnp.swapaxes(jnp.concatenate([bt, bt, kt, kt], axis=1), 1, 2)
    bk_t = jnp.where(bk_mask, bk_t, jnp.zeros_like(bk_t))
    vw = stack_heads(entries(v_n))
    gcol = jnp.swapaxes(
        jnp.stack([gam_n[ci * 8:(ci + 1) * 8, p * LANES:(p + 1) * LANES]
                   for ci in range(n_chunks) for p in range(n_pairs)]), 1, 2)[:, :, 7:8]

    gm = bmm(jnp.concatenate([at, rt], axis=1), bk_t)
    a_ab = jnp.where(strict, gm[:, :ck, :LANES], 0.0)
    a_ak = jnp.where(strict, gm[:, :ck, LANES:], 0.0).astype(BF16)
    a_rb = jnp.where(incl, gm[:, ck:, :LANES], 0.0).astype(BF16)
    a_rk = jnp.where(incl, gm[:, ck:, LANES:], 0.0).astype(BF16)
    zw_t, za_t, g_new = stage_lora()
    new_chunks = []
    stage_order = iter(range(n_chunks))

    def stage_next():
        ci = next(stage_order, None)
        if ci is not None:
            new_chunks.append(stage_chunk(ci, zw_t, za_t))

    stage_next()
    tm_ = eye_cat + a_ab
    pw = bmm(a_ab.astype(BF16), stack_heads(a_ab))
    for step in range(4):
        both = bmm(jnp.concatenate([tm_, pw], axis=1).astype(BF16), stack_heads(pw))
        tm_ = tm_ + both[:, :ck, :]
        pw = both[:, ck:, :]
        if step % 2 == 0:
            stage_next()
    tm_ = tm_ + bmm(tm_.astype(BF16), stack_heads(pw))
    t_cat = tm_.astype(BF16)
    by_v = bmm(jnp.concatenate([a_ak, a_rk, bk_t[:, :, LANES:]], axis=1), vw)
    ta = bmm(t_cat, stack_heads(at))
    for _ in range(n_chunks):
        stage_next()
    tx = bmm(t_cat, stack_heads(by_v[:, :ck, :]))
    yk = by_v[:, ck:2 * ck, :]
    dk = by_v[:, 2 * ck:, :]
    new_staged = [jnp.concatenate(parts, axis=0) for parts in zip(*new_chunks)] + [g_new]
    tar = jnp.concatenate([ta.astype(BF16), rt], axis=1)
    ba = jnp.concatenate([bk_t[:, :, :LANES], a_rb], axis=1)

    s = state_ref[...]
    y_rows = []
    for ci in range(n_chunks):
        es = slice(ci * n_pairs, (ci + 1) * n_pairs)
        o1 = bmm(tar[es], s.astype(BF16))
        u = o1[:, :ck, :] + tx[es]
        o2 = bmm(ba[es], stack_heads(u))
        y = o1[:, ck:, :] + o2[:, LANES:, :] + yk[es]
        y_rows.append(jnp.concatenate([y[p] for p in range(n_pairs)], axis=1))
        s = (s + o2[:, :LANES, :] + dk[es]) * gcol[es]
    state_ref[...] = s

    y = jnp.concatenate(y_rows, axis=0)
    inv_n = 1.0 / HEAD_DIM
    mean = _seg_sum(y, seg_ones) * inv_n
    ycen = y - mean
    var = _seg_sum(ycen * ycen, seg_ones) * inv_n
    yn = ycen * lax.rsqrt(var + GN_EPS) * gnw_ref[...] + gnb_ref[...]
    o_ref[...] = ((yn + bonus_p) * g_p).astype(o_ref.dtype)

    for ref, val in zip(staged, new_staged):
        ref[...] = val


def _wkv(rr, rk, rv, rl, w0, w2p, a0, a2p, g2, k_k, k_a, r_k, gn_w, gn_b, *, batch, seq, tt):
    t_rows, c = rr.shape
    steps = seq // tt
    row_in = lambda b, j: (b * steps + jnp.minimum(j, steps - 1), 0)
    row_out = lambda b, j: (b * steps + jnp.maximum(j - 1, 0), 0)
    vec = _const_spec((1, c))
    n_gam = (tt // WKV_CHUNK) * 8
    return pl.pallas_call(
        _wkv_kernel,
        out_shape=jax.ShapeDtypeStruct((t_rows, c), BF16),
        grid=(batch, steps + 1),
        in_specs=[
            pl.BlockSpec((tt, c), row_in), pl.BlockSpec((tt, c), row_in), pl.BlockSpec((tt, c), row_in),
            pl.BlockSpec((tt, rl.shape[1]), row_in),
            vec, _const_spec(w2p.shape), vec, _const_spec(a2p.shape), _const_spec(g2.shape),
            vec, vec, vec, vec, vec,
        ],
        out_specs=pl.BlockSpec((tt, c), row_out),
        scratch_shapes=[pltpu.VMEM((c // LANES, LANES, LANES), F32)]
        + [pltpu.VMEM((tt, c), BF16)] * 5
        + [pltpu.VMEM((n_gam, c), F32), pltpu.VMEM((tt, c), F32), pltpu.VMEM((tt, c), F32)],
        compiler_params=pltpu.CompilerParams(
            dimension_semantics=("arbitrary", "arbitrary"), vmem_limit_bytes=VMEM_LIMIT),
        name="wkv",
    )(rr, rk, rv, rl, w0, w2p, a0, a2p, g2, k_k, k_a, r_k, gn_w, gn_b)


def _branch_kernel(o0_ref, l0_ref, o1_ref, l1_ref, o2_ref, l2_ref, gate_ref, orw_ref, x_ref,
                   wb_ref, wo_ref, gt_ref, sc_ref, sh_ref, gpost_ref, gpre_ref,
                   x1_ref, h2_ref, nat_ref):
    tm, d = x_ref.shape
    gw = GROUP_WIDTH

    def natural(ref, dil, slab):
        if dil == 1:
            return ref[...].astype(F32)
        for r in range(dil):
            for s in range(gw // LANES):
                nat_ref[slab + s, pl.ds(r, tm // dil, stride=dil), :] = (
                    ref[:, r * gw + s * LANES:r * gw + (s + 1) * LANES].astype(F32))
        return jnp.concatenate([nat_ref[slab + s] for s in range(gw // LANES)], axis=1)

    dils = [dil for _, dil in ATTN_GROUPS]
    per_group = gw // LANES
    o0, o1, o2 = [natural(ref, dil, 2 * g * per_group)
                  for g, (ref, dil) in enumerate(zip((o0_ref, o1_ref, o2_ref), dils))]
    l0, l1, l2 = [natural(ref, dil, (2 * g + 1) * per_group)
                  for g, (ref, dil) in enumerate(zip((l0_ref, l1_ref, l2_ref), dils))]
    mx = jnp.maximum(jnp.maximum(l0, l1), l2)
    e0, e1, e2 = jnp.exp(l0 - mx), jnp.exp(l1 - mx), jnp.exp(l2 - mx)
    o_attn = (e0 * o0 + e1 * o1 + e2 * o2) / (e0 + e1 + e2)
    ya = _dot(o_attn.astype(BF16), wb_ref[0:gw, :].astype(BF16))
    yr = _dot(orw_ref[...], wb_ref[gw:, :].astype(BF16))
    merged = gate_ref[:, 0:d].astype(F32) * ya + gate_ref[:, d:2 * d].astype(F32) * yr
    mix = _dot(merged.astype(BF16), wo_ref[...].astype(BF16))
    x1 = x_ref[...] + gt_ref[...] * (_rms(mix) * gpost_ref[...])
    x1_ref[...] = x1
    h2 = _rms(x1) * gpre_ref[...]
    h2_ref[...] = (h2 * (1.0 + sc_ref[...]) + sh_ref[...]).astype(BF16)


def _branch(attn_outs, gates, o_rwkv, x2, wb, wo, gt, sc, sh, gpost, gpre, *, seq, tm):
    t_rows, d = x2.shape
    tiles_per_seq = seq // tm
    row = lambda i: (i, 0)
    per_batch = lambda i: (i // tiles_per_seq, 0, 0)
    gw = GROUP_WIDTH
    attn_specs = [pl.BlockSpec((tm // dil, dil * gw), row) for _, dil in ATTN_GROUPS for _ in range(2)]
    n_slabs = 2 * len(ATTN_GROUPS) * (gw // LANES)
    return pl.pallas_call(
        _branch_kernel,
        out_shape=[jax.ShapeDtypeStruct((t_rows, d), F32), jax.ShapeDtypeStruct((t_rows, d), BF16)],
        grid=(t_rows // tm,),
        scratch_shapes=[pltpu.VMEM((n_slabs, tm, LANES), F32)],
        in_specs=attn_specs + [
            pl.BlockSpec((tm, 2 * d), row), pl.BlockSpec((tm, d), row), pl.BlockSpec((tm, d), row),
            _const_spec(wb.shape), _const_spec(wo.shape),
            pl.BlockSpec((None, 1, d), per_batch), pl.BlockSpec((None, 1, d), per_batch),
            pl.BlockSpec((None, 1, d), per_batch),
            _const_spec((1, d)), _const_spec((1, d)),
        ],
        out_specs=[pl.BlockSpec((tm, d), row), pl.BlockSpec((tm, d), row)],
        compiler_params=pltpu.CompilerParams(
            dimension_semantics=("arbitrary",), vmem_limit_bytes=VMEM_LIMIT),
        name="branch",
    )(*attn_outs, gates, o_rwkv, x2, wb, wo, gt, sc, sh, gpost, gpre)


def _ffn_kernel(h_ref, x_ref, w1_ref, w2_ref, gt_ref, gpost_ref, o_ref, *, tf):
    h = h_ref[...]
    acc = jnp.zeros(x_ref.shape, F32)
    for j in range(w1_ref.shape[1] // tf):
        a = jnp.maximum(_dot(h, w1_ref[:, j * tf:(j + 1) * tf].astype(BF16)), 0.0)
        acc = acc + _dot((a * a).astype(BF16), w2_ref[j * tf:(j + 1) * tf, :].astype(BF16))
    o_ref[...] = x_ref[...] + gt_ref[...] * (_rms(acc) * gpost_ref[...])


def _ffn(h2, x1, w1, w2, gt, gpost, *, seq, tm, tf):
    t_rows, d = x1.shape
    tiles_per_seq = seq // tm
    row = lambda i: (i, 0)
    per_batch = lambda i: (i // tiles_per_seq, 0, 0)
    return pl.pallas_call(
        functools.partial(_ffn_kernel, tf=tf),
        out_shape=jax.ShapeDtypeStruct((t_rows, d), F32),
        grid=(t_rows // tm,),
        in_specs=[
            pl.BlockSpec((tm, d), row), pl.BlockSpec((tm, d), row),
            _const_spec(w1.shape), _const_spec(w2.shape),
            pl.BlockSpec((None, 1, d), per_batch), _const_spec((1, d)),
        ],
        out_specs=pl.BlockSpec((tm, d), row),
        compiler_params=pltpu.CompilerParams(
            dimension_semantics=("arbitrary",), vmem_limit_bytes=VMEM_LIMIT),
        name="ffn",
    )(h2, x1, w1, w2, gt, gpost)


def kernel(x, c, positions, ada_w, ada_b, norm_mix_pre, norm_mix_post, norm_ffn_pre, norm_ffn_post, w_in, shift_mu, decay_w0, decay_w2, iclr_a0, iclr_a2, gate_g2, k_k, k_a, r_k, gn_w, gn_b, w_branch, w_out, w_ff1, w_ff2):
    batch, seq, d = x.shape
    depth = ada_w.shape[0]
    t_rows = batch * seq
    shift_w = shift_mu.shape[1]
    assert shift_w == 3 * d + DECAY_LORA + ICLR_LORA + GATE_LORA
    assert DECAY_LORA + ICLR_LORA == LANES and GATE_LORA == LANES
    tm = TOKEN_TILE
    assert seq % tm == 0 and seq % WKV_TILE == 0 and WKV_TILE % WKV_CHUNK == 0

    x2 = x.reshape(t_rows, d)
    pos_rows = positions.astype(F32).reshape(t_rows // tm, 1, tm)
    half = ROPE_DIM // 2
    inv_col = (ROPE_THETA ** (-jnp.arange(half, dtype=F32) * 2.0 / ROPE_DIM)).reshape(half, 1)
    c_pad = jnp.pad(c, ((0, 8 - batch), (0, 0)))
    vec = lambda p: p.reshape(1, -1)

    for l in range(depth):
        mod, w_in_b = _mod(c_pad, ada_w[l], ada_b[l].reshape(1, -1), w_in[l])
        sh1, sc1, gt1, sh2, sc2, gt2 = [m.reshape(batch, 1, d) for m in jnp.split(mod[:batch], 6, axis=-1)]

        a0, a1, a2, rr, rk, rv, rl, gates = _inproj(
            x2, pos_rows, sc1, sh1, vec(norm_mix_pre[l]), inv_col, w_in_b, vec(shift_mu[l]),
            seq=seq, tm=tm)

        attn_outs = _attn([a0, a1, a2], batch=batch, seq=seq)

        zeros = jnp.zeros((DECAY_LORA, d), BF16)
        w2p = jnp.concatenate([decay_w2[l].astype(BF16), zeros], axis=0)
        a2p = jnp.concatenate([zeros, iclr_a2[l].astype(BF16)], axis=0)
        o_rwkv = _wkv(rr, rk, rv, rl, vec(decay_w0[l]), w2p, vec(iclr_a0[l]), a2p,
                      gate_g2[l].astype(BF16), vec(k_k[l]), vec(k_a[l]), vec(r_k[l]),
                      vec(gn_w[l]), vec(gn_b[l]), batch=batch, seq=seq, tt=WKV_TILE)

        x1, h2 = _branch(attn_outs, gates, o_rwkv, x2, w_branch[l],
                         w_out[l], gt1, sc2, sh2, vec(norm_mix_post[l]),
                         vec(norm_ffn_pre[l]), seq=seq, tm=tm)
        x2 = _ffn(h2, x1, w_ff1[l], w_ff2[l], gt2, vec(norm_ffn_post[l]),
                  seq=seq, tm=tm, tf=FFN_CHUNK)
    return x2.reshape(batch, seq, d)
```
